```python
import math
import jax, jax.numpy as jnp
from jax import lax
import numpy as np

D_MODEL = 1024
BATCH = 2
SEQ = 8192
DEPTH = 2
DEC_BATCH = 32
DEC_SEQ = 4
PAST_LEN = 8192
PAGE_SIZE = 128

N_A_LAYERS = DEPTH // 2
N_B_LAYERS = DEPTH - N_A_LAYERS
HEAD_DIM = 64
C_CONV = 3 * D_MODEL // 4
CONV_WIDTH = 31
N_SB_HEADS = C_CONV // HEAD_DIM
SB_WIDTH = N_SB_HEADS * HEAD_DIM
SB_BIAS_INIT = -7.0
N_MEM_HEADS = 4
MEM_WIDTH = N_MEM_HEADS * HEAD_DIM
N_MEM = 256
Q_BLOCK = 128
PEER_HEADS = 8
PEER_NKEYS = 128
PEER_N_EXPERTS = PEER_NKEYS * PEER_NKEYS
PEER_TOPK = 16
PEER_DKEY = D_MODEL // 4
PEER_DHALF = PEER_DKEY // 2
PEER_CHUNK = 512
EPS = 1e-6

kernel_name = 'yoco_conformer_stickbreak_peer_step'


def rms_norm(x, g):
    xf = x.astype(jnp.float32)
    y = xf * lax.rsqrt(jnp.mean(xf * xf, axis=-1, keepdims=True) + EPS)
    return (y * g.astype(jnp.float32)).astype(x.dtype)


def layer_norm(x, g, b):
    xf = x.astype(jnp.float32)
    mu = jnp.mean(xf, axis=-1, keepdims=True)
    var = jnp.mean(jnp.square(xf - mu), axis=-1, keepdims=True)
    y = (xf - mu) * lax.rsqrt(var + EPS)
    return (y * g.astype(jnp.float32) + b.astype(jnp.float32)).astype(x.dtype)


def memory_kv(mem, g_mem_norm, w_mem_kv, g_mem_k):
    b, m, _ = mem.shape
    kv = rms_norm(mem, g_mem_norm) @ w_mem_kv
    k = rms_norm(kv[..., :MEM_WIDTH].reshape(b, m, N_MEM_HEADS, HEAD_DIM), g_mem_k)
    v = kv[..., MEM_WIDTH:].reshape(b, m, N_MEM_HEADS, HEAD_DIM)
    return k, v


def mem_attend(q, k, v):
    s = jnp.einsum('bthd,bmhd->bhtm', q, k).astype(jnp.float32) / math.sqrt(HEAD_DIM)
    p = jax.nn.softmax(s, axis=-1)
    o = jnp.einsum('bhtm,bmhd->bthd', p, v.astype(jnp.float32)).astype(v.dtype)
    return o.reshape(q.shape[0], q.shape[1], MEM_WIDTH)


def conformer_conv(u, prev, w_dw, b_dw, g_ln, b_ln):
    a, gate = jnp.split(u, 2, axis=-1)
    glu = a * jax.nn.sigmoid(gate)
    padded = jnp.concatenate([prev.astype(glu.dtype), glu], axis=1)
    y = lax.conv_general_dilated(padded, w_dw[:, None, :], window_strides=(1,), padding='VALID',
                                 dimension_numbers=('NWC', 'WIO', 'NWC'),
                                 feature_group_count=C_CONV) + b_dw
    y = jax.nn.silu(layer_norm(y, g_ln, b_ln))
    return y, padded[:, -(CONV_WIDTH - 1):]


def stick_breaking(q, k, v, bias, q_pos, k_pos):
    z = jnp.einsum('bqhd,bkhd->bhqk', q, k).astype(jnp.float32) / math.sqrt(HEAD_DIM)
    z = z + bias.astype(jnp.float32)[None, :, None, None]
    valid = k_pos[None, :] < q_pos[:, None]
    log_keep = jnp.where(valid, jax.nn.log_sigmoid(-z), 0.0)
    log_after = lax.cumsum(log_keep, axis=3, reverse=True) - log_keep
    a = jnp.where(valid, jnp.exp(jax.nn.log_sigmoid(z) + log_after), 0.0)
    return jnp.einsum('bhqk,bkhd->bqhd', a, v.astype(jnp.float32)).astype(v.dtype)


def stick_breaking_prompt(q, k, v, bias):
    t = q.shape[1]
    outs = []
    for i in range(t // Q_BLOCK):
        s0, s1 = i * Q_BLOCK, (i + 1) * Q_BLOCK
        outs.append(stick_breaking(q[:, s0:s1], k[:, :s1], v[:, :s1], bias,
                                   jnp.arange(s0, s1), jnp.arange(s1)))
    return jnp.concatenate(outs, axis=1)


def shared_kv(x, g_kv, w_kv, g_sb_k):
    b, t, _ = x.shape
    kv = rms_norm(x, g_kv) @ w_kv
    k = rms_norm(kv[..., :SB_WIDTH].reshape(b, t, N_SB_HEADS, HEAD_DIM), g_sb_k)
    v = kv[..., SB_WIDTH:].reshape(b, t, N_SB_HEADS, HEAD_DIM)
    return k, v


def peer(h, w_q, sub_keys, u_tab, v_tab):
    shp = h.shape
    hf = h.reshape(-1, shp[-1])
    n = hf.shape[0]
    n_chunks = n // PEER_CHUNK if (n % PEER_CHUNK == 0 and n > PEER_CHUNK) else 1

    def one(xc):
        m = xc.shape[0]
        q = (xc @ w_q).reshape(m, PEER_HEADS, 2, PEER_DHALF)
        s = jnp.einsum('mhpd,hpkd->mhpk', q, sub_keys).astype(jnp.float32)
        sv, si = lax.top_k(s, PEER_TOPK)
        cand_s = (sv[:, :, 0, :, None] + sv[:, :, 1, None, :]).reshape(m, PEER_HEADS, PEER_TOPK * PEER_TOPK)
        cand_i = (si[:, :, 0, :, None] * PEER_NKEYS + si[:, :, 1, None, :]).reshape(m, PEER_HEADS, PEER_TOPK * PEER_TOPK)
        top_s, top_j = lax.top_k(cand_s, PEER_TOPK)
        idx = jnp.take_along_axis(cand_i, top_j, axis=-1)
        gate = jax.nn.softmax(top_s, axis=-1)
        u = jnp.take(u_tab, idx, axis=0)
        act = jax.nn.gelu(jnp.einsum('mhkd,md->mhk', u, xc).astype(jnp.float32), approximate=False)
        wgt = (gate * act).astype(xc.dtype)
        v = jnp.take(v_tab, idx, axis=0)
        return jnp.einsum('mhk,mhkd->md', wgt, v)

    out = lax.map(one, hf.reshape(n_chunks, n // n_chunks, shp[-1]))
    return out.reshape(shp)


def forward(x, conv_prev, mem_k, mem_v, k_past, v_past,
            g_norm1, g_norm2, g_mem_q, w_in_a, w_dw, b_dw, g_conv_ln, b_conv_ln, w_out_a,
            g_kv, w_kv, g_sb_k, w_in_b, g_sb_q, sb_bias, w_out_b,
            peer_w_q, peer_sub_keys, peer_u, peer_v):
    b, t, _ = x.shape
    conv_new = []
    k_sh = v_sh = None
    for l in range(DEPTH):
        h = rms_norm(x, g_norm1[l])
        if l < N_A_LAYERS:
            p = h @ w_in_a[l]
            c_out, st = conformer_conv(p[..., :2 * C_CONV], conv_prev[l], w_dw[l], b_dw[l],
                                       g_conv_ln[l], b_conv_ln[l])
            conv_new.append(st)
            qm = p[..., 2 * C_CONV:]
            w_out = w_out_a[l]
        else:
            j = l - N_A_LAYERS
            if j == 0:
                k_sh, v_sh = shared_kv(x, g_kv, w_kv, g_sb_k)
            p = h @ w_in_b[j]
            q = rms_norm(p[..., :SB_WIDTH].reshape(b, t, N_SB_HEADS, HEAD_DIM), g_sb_q[j])
            if k_past is None:
                o = stick_breaking_prompt(q, k_sh, v_sh, sb_bias[j])
            else:
                past = k_past.shape[1]
                k_all = jnp.concatenate([k_past.astype(k_sh.dtype), k_sh], axis=1)
                v_all = jnp.concatenate([v_past.astype(v_sh.dtype), v_sh], axis=1)
                o = stick_breaking(q, k_all, v_all, sb_bias[j], past + jnp.arange(t), jnp.arange(past + t))
            c_out = o.reshape(b, t, SB_WIDTH)
            qm = p[..., SB_WIDTH:]
            w_out = w_out_b[j]
        qm = rms_norm(qm.reshape(b, t, N_MEM_HEADS, HEAD_DIM), g_mem_q[l])
        m_out = mem_attend(qm, mem_k[l], mem_v[l])
        x = x + jnp.concatenate([c_out, m_out], axis=-1) @ w_out
        x = x + peer(rms_norm(x, g_norm2[l]), peer_w_q[l], peer_sub_keys[l], peer_u[l], peer_v[l])
    return x, jnp.stack(conv_new), k_sh, v_sh


def setup_inputs(seed: int = 0) -> dict:
    key = jax.random.key(seed)
    ks = jax.random.split(key, 40)
    f32 = jnp.float32
    n_pages = PAST_LEN // PAGE_SIZE
    n_used = DEC_BATCH * n_pages
    n_pool = n_used + max(1, n_used // 4)

    def nrm(k, shape, scale):
        return jax.random.normal(k, shape, f32) * scale

    def gain(k, shape):
        return 1.0 + 0.02 * jax.random.normal(k, shape, f32)

    page_table = jax.random.permutation(ks[0], n_pool)[:n_used].reshape(DEC_BATCH, n_pages).astype(jnp.int32)
    d_in_a = 2 * C_CONV + MEM_WIDTH
    d_cat_a = C_CONV + MEM_WIDTH
    d_in_b = SB_WIDTH + MEM_WIDTH
    return {
        'x_prompt': nrm(ks[1], (BATCH, SEQ, D_MODEL), 1.0),
        'x_sample': nrm(ks[2], (DEC_BATCH, DEC_SEQ, D_MODEL), 1.0),
        'mem_prompt': nrm(ks[3], (BATCH, N_MEM, D_MODEL), 1.0),
        'state_conv': nrm(ks[4], (N_A_LAYERS, DEC_BATCH, CONV_WIDTH - 1, C_CONV), 0.5),
        'cache_k': nrm(ks[5], (n_pool, PAGE_SIZE, N_SB_HEADS, HEAD_DIM), 1.0),
        'cache_v': nrm(ks[6], (n_pool, PAGE_SIZE, N_SB_HEADS, HEAD_DIM), 1.0),
        'cache_mem_k': nrm(ks[7], (DEPTH, DEC_BATCH, N_MEM, N_MEM_HEADS, HEAD_DIM), 1.0),
        'cache_mem_v': nrm(ks[8], (DEPTH, DEC_BATCH, N_MEM, N_MEM_HEADS, HEAD_DIM), 1.0),
        'page_table': page_table,
        'g_norm1': gain(ks[9], (DEPTH, D_MODEL)),
        'g_norm2': gain(ks[10], (DEPTH, D_MODEL)),
        'g_mem_norm': gain(ks[11], (DEPTH, D_MODEL)),
        'w_mem_kv': nrm(ks[12], (DEPTH, D_MODEL, 2 * MEM_WIDTH), D_MODEL ** -0.5),
        'g_mem_q': gain(ks[13], (DEPTH, HEAD_DIM)),
        'g_mem_k': gain(ks[14], (DEPTH, HEAD_DIM)),
        'w_in_a': nrm(ks[15], (N_A_LAYERS, D_MODEL, d_in_a), D_MODEL ** -0.5),
        'w_dw': nrm(ks[16], (N_A_LAYERS, CONV_WIDTH, C_CONV), CONV_WIDTH ** -0.5),
        'b_dw': nrm(ks[17], (N_A_LAYERS, C_CONV), 0.02),
        'g_conv_ln': gain(ks[18], (N_A_LAYERS, C_CONV)),
        'b_conv_ln': nrm(ks[19], (N_A_LAYERS, C_CONV), 0.02),
        'w_out_a': nrm(ks[20], (N_A_LAYERS, d_cat_a, D_MODEL), d_cat_a ** -0.5),
        'g_kv': gain(ks[21], (D_MODEL,)),
        'w_kv': nrm(ks[22], (D_MODEL, 2 * SB_WIDTH), D_MODEL ** -0.5),
        'g_sb_k': gain(ks[23], (HEAD_DIM,)),
        'w_in_b': nrm(ks[24], (N_B_LAYERS, D_MODEL, d_in_b), D_MODEL ** -0.5),
        'g_sb_q': gain(ks[25], (N_B_LAYERS, HEAD_DIM)),
        'sb_bias': SB_BIAS_INIT + nrm(ks[31], (N_B_LAYERS, N_SB_HEADS), 0.1),
        'w_out_b': nrm(ks[26], (N_B_LAYERS, d_in_b, D_MODEL), d_in_b ** -0.5),
        'peer_w_q': nrm(ks[27], (DEPTH, D_MODEL, PEER_HEADS * PEER_DKEY), D_MODEL ** -0.5),
        'peer_sub_keys': nrm(ks[28], (DEPTH, PEER_HEADS, 2, PEER_NKEYS, PEER_DHALF), PEER_DHALF ** -0.5),
        'peer_u': nrm(ks[29], (DEPTH, PEER_N_EXPERTS, D_MODEL), D_MODEL ** -0.5),
        'peer_v': nrm(ks[30], (DEPTH, PEER_N_EXPERTS, D_MODEL), PEER_HEADS ** -0.5),
    }


def reference(x_prompt, x_sample, mem_prompt, state_conv, cache_k, cache_v, cache_mem_k, cache_mem_v,
              page_table, g_norm1, g_norm2, g_mem_norm, w_mem_kv, g_mem_q, g_mem_k,
              w_in_a, w_dw, b_dw, g_conv_ln, b_conv_ln, w_out_a, g_kv, w_kv, g_sb_k,
              w_in_b, g_sb_q, sb_bias, w_out_b, peer_w_q, peer_sub_keys, peer_u, peer_v):
    weights = dict(g_norm1=g_norm1, g_norm2=g_norm2, g_mem_q=g_mem_q, w_in_a=w_in_a, w_dw=w_dw,
                   b_dw=b_dw, g_conv_ln=g_conv_ln, b_conv_ln=b_conv_ln, w_out_a=w_out_a,
                   g_kv=g_kv, w_kv=w_kv, g_sb_k=g_sb_k, w_in_b=w_in_b, g_sb_q=g_sb_q,
                   sb_bias=sb_bias, w_out_b=w_out_b, peer_w_q=peer_w_q, peer_sub_keys=peer_sub_keys,
                   peer_u=peer_u, peer_v=peer_v)
    mks, mvs = [], []
    for l in range(DEPTH):
        mk, mv = memory_kv(mem_prompt, g_mem_norm[l], w_mem_kv[l], g_mem_k[l])
        mks.append(mk)
        mvs.append(mv)
    mem_k_prompt = jnp.stack(mks)
    mem_v_prompt = jnp.stack(mvs)
    conv0 = jnp.zeros((N_A_LAYERS, x_prompt.shape[0], CONV_WIDTH - 1, C_CONV), x_prompt.dtype)
    y_prompt, conv_prompt, k_prompt, v_prompt = forward(
        x_prompt, conv0, mem_k_prompt, mem_v_prompt, None, None, **weights)
    n_seq, n_pages = page_table.shape
    k_past = cache_k[page_table].reshape(n_seq, n_pages * PAGE_SIZE, N_SB_HEADS, HEAD_DIM)
    v_past = cache_v[page_table].reshape(n_seq, n_pages * PAGE_SIZE, N_SB_HEADS, HEAD_DIM)
    y_sample, conv_sample, k_sample, v_sample = forward(
        x_sample, state_conv, cache_mem_k, cache_mem_v, k_past, v_past, **weights)
    return (y_prompt, y_sample, conv_prompt, conv_sample, k_prompt, v_prompt, k_sample, v_sample,
            mem_k_prompt, mem_v_prompt)
```

```python
import functools

import jax
import jax.numpy as jnp
from jax import lax
from jax.experimental import pallas as pl
from jax.experimental.pallas import tpu as pltpu

F32 = jnp.float32
BF16 = jnp.bfloat16

HEAD_DIM = 64
EPS = 1e-6
TOPK = 16
NKEYS = 128
LANES = 128
VMEM_LIMIT = 56 * 1024 * 1024
NEG_INF = float("-inf")


def _params(*sem):
    return pltpu.CompilerParams(dimension_semantics=sem, vmem_limit_bytes=VMEM_LIMIT)


def _dot(a, b):
    return jnp.dot(a, b, preferred_element_type=F32)


def _dot_nt(a, b):
    return lax.dot_general(a, b, (((1,), (1,)), ((), ())), preferred_element_type=F32)


def _split(x):
    hi = x.astype(BF16)
    return hi, (x - hi.astype(F32)).astype(BF16)


def _rms(x, g):
    return x * lax.rsqrt(jnp.mean(x * x, axis=-1, keepdims=True) + EPS) * g


def _head_rms(x, g, ind):
    hi, lo = _split(x * x)
    ms = (_dot(hi, ind) + _dot(lo, ind)) * (1.0 / HEAD_DIM)
    return x * lax.rsqrt(ms + EPS) * g


def _neg_softplus(z):
    return -(jnp.maximum(z, 0.0) + jnp.log1p(jnp.exp(-jnp.abs(z))))


def _full(shape):
    n = len(shape)
    return pl.BlockSpec(shape, lambda *_: (0,) * n)


def _memkv_kernel(mem_ref, g_ref, w_ref, gk_ref, ind_ref, k_ref, v_ref, *, width):
    h = _rms(mem_ref[...], g_ref[0]).astype(BF16)
    kv = _dot(h, w_ref[0])
    k_ref[0] = _head_rms(kv[:, :width], gk_ref[0], ind_ref[...])
    v_ref[0] = kv[:, width:]


def _memory_kv(mem, g_norm, w_kv, g_k, ind):
    depth, d, w2 = w_kv.shape
    width = w2 // 2
    n = mem.shape[0]
    return pl.pallas_call(
        functools.partial(_memkv_kernel, width=width),
        grid=(depth,),
        in_specs=[_full((n, d)),
                  pl.BlockSpec((1, 1, d), lambda l: (l, 0, 0)),
                  pl.BlockSpec((1, d, w2), lambda l: (l, 0, 0)),
                  pl.BlockSpec((1, 1, width), lambda l: (l, 0, 0)),
                  _full((width, width))],
        out_specs=[pl.BlockSpec((1, n, width), lambda l: (l, 0, 0))] * 2,
        out_shape=[jax.ShapeDtypeStruct((depth, n, width), F32)] * 2,
        compiler_params=_params("arbitrary"),
        name="memory_kv",
    )(mem, g_norm, w_kv, g_k, ind)


def _proj_a_kernel(x_ref, g_ref, w_ref, gq_ref, ind_ref, glu_ref, qm_ref, *, c):
    h = _rms(x_ref[0], g_ref[...]).astype(BF16)
    p = _dot(h, w_ref[...])
    glu_ref[0] = p[:, :c] * jax.nn.sigmoid(p[:, c:2 * c])
    qm_ref[0] = _head_rms(p[:, 2 * c:], gq_ref[...], ind_ref[...])


def _proj_a(x, g, w, gq, ind, c, tt):
    b, t, d = x.shape
    wm = w.shape[1] - 2 * c
    tile = lambda width: pl.BlockSpec((1, tt, width), lambda i, j: (i, j, 0))
    return pl.pallas_call(
        functools.partial(_proj_a_kernel, c=c),
        grid=(b, t // tt),
        in_specs=[tile(d), _full((1, d)), _full(w.shape), _full((1, wm)), _full((wm, wm))],
        out_specs=[tile(c), tile(wm)],
        out_shape=[jax.ShapeDtypeStruct((b, t, c), F32), jax.ShapeDtypeStruct((b, t, wm), F32)],
        compiler_params=_params("arbitrary", "arbitrary"),
        name="in_proj_a",
    )(x, g, w, gq, ind)


def _proj_b_kernel(x_ref, gkv_ref, wkv_ref, gk_ref, g1_ref, wb_ref, gq_ref, gm_ref, ind_sb_ref, ind_m_ref,
                   k_ref, v_ref, q_ref, qm_ref, *, sb):
    x = x_ref[0]
    kv = _dot(_rms(x, gkv_ref[...]).astype(BF16), wkv_ref[...])
    k_ref[0] = _head_rms(kv[:, :sb], gk_ref[...], ind_sb_ref[...])
    v_ref[0] = kv[:, sb:]
    p = _dot(_rms(x, g1_ref[...]).astype(BF16), wb_ref[...])
    q_ref[0] = _head_rms(p[:, :sb], gq_ref[...], ind_sb_ref[...])
    qm_ref[0] = _head_rms(p[:, sb:], gm_ref[...], ind_m_ref[...])


def _proj_b(x, gkv, wkv, gk, g1, wb, gq, gm, ind_sb, ind_m, sb, tt):
    b, t, d = x.shape
    wm = wb.shape[1] - sb
    tile = lambda width: pl.BlockSpec((1, tt, width), lambda i, j: (i, j, 0))
    sds = lambda width: jax.ShapeDtypeStruct((b, t, width), F32)
    return pl.pallas_call(
        functools.partial(_proj_b_kernel, sb=sb),
        grid=(b, t // tt),
        in_specs=[tile(d), _full((1, d)), _full(wkv.shape), _full((1, sb)), _full((1, d)), _full(wb.shape),
                  _full((1, sb)), _full((1, wm)), _full((sb, sb)), _full((wm, wm))],
        out_specs=[tile(sb), tile(sb), tile(sb), tile(wm)],
        out_shape=[sds(sb), sds(sb), sds(sb), sds(wm)],
        compiler_params=_params("arbitrary", "arbitrary"),
        name="in_proj_b",
    )(x, gkv, wkv, gk, g1, wb, gq, gm, ind_sb, ind_m)


CONV_HALO = 32
CONV_ROWS = 64


def _conv_kernel(glu_ref, prev_ref, w_ref, b_ref, g_ref, bl_ref, out_ref, pad_ref, y_ref, *, tt, width, carry):
    c = glu_ref.shape[-1]

    @pl.when(pl.program_id(1) == 0)
    def _():
        pad_ref[0:CONV_HALO, :] = prev_ref[0]

    pad_ref[CONV_HALO:CONV_HALO + tt, :] = glu_ref[0]
    first = CONV_HALO - (width - 1)
    rows = min(CONV_ROWS, tt)
    for r0 in range(0, tt, rows):
        for l0 in range(0, c, LANES):
            acc = jnp.zeros((rows, LANES), F32) + b_ref[:, l0:l0 + LANES]
            for k in range(width):
                acc = acc + w_ref[k:k + 1, l0:l0 + LANES] * pad_ref[first + k + r0:first + k + r0 + rows,
                                                                    l0:l0 + LANES]
            y_ref[r0:r0 + rows, l0:l0 + LANES] = acc
    y = y_ref[...]
    mu = jnp.mean(y, axis=-1, keepdims=True)
    var = jnp.mean(jnp.square(y - mu), axis=-1, keepdims=True)
    yn = (y - mu) * lax.rsqrt(var + EPS) * g_ref[...] + bl_ref[...]
    out_ref[0] = yn * jax.nn.sigmoid(yn)
    if carry:
        pad_ref[0:CONV_HALO, :] = pad_ref[tt:tt + CONV_HALO, :]


def _conv(glu, prev, w, bias, g_ln, b_ln, tt):
    b, t, c = glu.shape
    width = w.shape[0]
    tile = pl.BlockSpec((1, tt, c), lambda i, j: (i, j, 0))
    return pl.pallas_call(
        functools.partial(_conv_kernel, tt=tt, width=width, carry=t > tt),
        grid=(b, t // tt),
        in_specs=[tile, pl.BlockSpec((1, CONV_HALO, c), lambda i, j: (i, 0, 0)), _full(w.shape),
                  _full((1, c)), _full((1, c)), _full((1, c))],
        out_specs=tile,
        out_shape=jax.ShapeDtypeStruct((b, t, c), F32),
        scratch_shapes=[pltpu.VMEM((CONV_HALO + tt, c), F32), pltpu.VMEM((tt, c), F32)],
        compiler_params=_params("arbitrary", "arbitrary"),
        name="conformer_conv",
    )(glu, prev, w, bias, g_ln, b_ln)


def _mix_kernel(x_ref, c_ref, qm_ref, mk_ref, mv_ref, wc_ref, wm_ref, out_ref, *, heads):
    qm = qm_ref[0]
    mk = mk_ref[0].astype(BF16)
    mv = mv_ref[0].astype(BF16)
    head_of_lane = lax.broadcasted_iota(jnp.int32, (1, qm.shape[-1]), 1) // HEAD_DIM
    m_out = jnp.zeros(qm.shape, F32)
    for h in range(heads):
        mine = head_of_lane == h
        s = _dot_nt(jnp.where(mine, qm, 0.0).astype(BF16), mk) * (HEAD_DIM ** -0.5)
        e = jnp.exp(s - jnp.max(s, axis=-1, keepdims=True))
        p = e / jnp.sum(e, axis=-1, keepdims=True)
        m_out = jnp.where(mine, _dot(p.astype(BF16), mv), m_out)
    out_ref[0] = (x_ref[0] + _dot(c_ref[0].astype(BF16), wc_ref[...])
                  + _dot(m_out.astype(BF16), wm_ref[...]))


def _mix(x, c_out, qm, mem_k, mem_v, wc, wm, tt):
    b, t, d = x.shape
    cw, mw = c_out.shape[-1], qm.shape[-1]
    n_mem = mem_k.shape[1]
    tile = lambda width: pl.BlockSpec((1, tt, width), lambda i, j: (i, j, 0))
    mem = pl.BlockSpec((1, n_mem, mw), lambda i, j: (i, 0, 0))
    return pl.pallas_call(
        functools.partial(_mix_kernel, heads=mw // HEAD_DIM),
        grid=(b, t // tt),
        in_specs=[tile(d), tile(cw), tile(mw), mem, mem, _full(wc.shape), _full(wm.shape)],
        out_specs=tile(d),
        out_shape=jax.ShapeDtypeStruct((b, t, d), F32),
        compiler_params=_params("arbitrary", "arbitrary"),
        name="mem_attn_out_proj",
    )(x, c_out, qm, mem_k, mem_v, wc, wm)


def _top16(s, key_idx):
    n = s.shape[0]
    work = s
    rank = jnp.full(s.shape, float(TOPK), F32)
    vals = []
    for r in range(TOPK):
        m = jnp.max(work, axis=0, keepdims=True)
        first = jnp.min(jnp.where(work == m, key_idx, float(n)), axis=0, keepdims=True)
        sel = key_idx == first
        rank = jnp.where(sel, float(r), rank)
        work = jnp.where(sel, NEG_INF, work)
        vals.append(m)
    return vals, rank


def _pair_top16(sv1, sv2, cols):
    sub = 8
    row = lax.broadcasted_iota(jnp.int32, (TOPK, cols), 0).astype(F32)
    sv2_col = jnp.zeros((TOPK, cols), F32)
    for c in range(TOPK):
        sv2_col = jnp.where(row == float(c), sv2[c], sv2_col)
    cands, flat = [], []
    for r in range(TOPK):
        nrow = TOPK if r == 0 else sub
        cands.append(jnp.where(row[:nrow] < float(TOPK // (r + 1)), sv1[r] + sv2_col[:nrow], NEG_INF))
        flat.append(row[:nrow] + float(TOPK * r))
    cand = jnp.concatenate(cands, axis=0)
    flat = jnp.concatenate(flat, axis=0)
    top = sv1[0] + sv2[0]
    kept = jnp.zeros(cand.shape, F32)
    z = jnp.zeros((1, cols), F32)
    for _ in range(TOPK):
        m = jnp.max(cand, axis=0, keepdims=True)
        first = jnp.min(jnp.where(cand == m, flat, float(TOPK * TOPK)), axis=0, keepdims=True)
        sel = flat == first
        kept = jnp.where(sel, 1.0, kept)
        cand = jnp.where(sel, NEG_INF, cand)
        z = z + jnp.exp(m - top)
    counts, o = [], 0
    for r in range(TOPK):
        nrow = TOPK if r == 0 else sub
        counts.append(jnp.sum(kept[o:o + nrow], axis=0, keepdims=True))
        o += nrow
    return counts, z


def _route_kernel(x_ref, g_ref, wqt_ref, keys_ref, hnt_ref, rho_ref, e2_ref, n1_ref, r1_ref, qt_ref, *, tt, heads):
    hnt = _rms(x_ref[...], g_ref[...]).T.astype(BF16)
    hnt_ref[...] = hnt
    qt_ref[...] = _dot(wqt_ref[...], hnt)
    key_idx = lax.broadcasted_iota(jnp.int32, (NKEYS, LANES), 0).astype(F32)

    def head(h, _):
        base = pl.multiple_of(h * 2 * NKEYS, 2 * NKEYS)
        for l0 in range(0, tt, LANES):
            ls = slice(l0, l0 + LANES)
            s1 = _dot(keys_ref[h, 0], qt_ref[pl.ds(base, NKEYS), ls].astype(BF16))
            s2 = _dot(keys_ref[h, 1], qt_ref[pl.ds(base + NKEYS, NKEYS), ls].astype(BF16))
            sv1, rank1 = _top16(s1, key_idx)
            sv2, rank2 = _top16(s2, key_idx)
            counts, z = _pair_top16(sv1, sv2, LANES)
            n1 = jnp.zeros(s1.shape, F32)
            for r in range(TOPK):
                n1 = jnp.where(rank1 == float(r), counts[r], n1)
            rho_ref[h, :, ls] = rank2
            n1_ref[h, :, ls] = n1
            e2_ref[h, :, ls] = jnp.exp(s2 - sv2[0]) / z
            r1_ref[h, :, ls] = jnp.exp(s1 - sv1[0])
        return 0

    lax.fori_loop(0, heads, head, 0)


def _route(x, g, wqt, keys, tt):
    n, d = x.shape
    heads = keys.shape[0]
    tab = pl.BlockSpec((heads, NKEYS, tt), lambda i: (0, 0, i))
    tab_shape = jax.ShapeDtypeStruct((heads, NKEYS, n), F32)
    return pl.pallas_call(
        functools.partial(_route_kernel, tt=tt, heads=heads),
        grid=(n // tt,),
        in_specs=[pl.BlockSpec((tt, d), lambda i: (i, 0)), _full((1, d)), _full(wqt.shape), _full(keys.shape)],
        out_specs=[pl.BlockSpec((d, tt), lambda i: (0, i)), tab, tab, tab, tab],
        out_shape=[jax.ShapeDtypeStruct((d, n), BF16), tab_shape, tab_shape, tab_shape, tab_shape],
        scratch_shapes=[pltpu.VMEM((wqt.shape[0], tt), F32)],
        compiler_params=_params("arbitrary"),
        name="peer_route",
    )(x, g, wqt, keys)


def _dense_kernel(hnt_ref, u_ref, vt_ref, rho_ref, e2_ref, n1_ref, r1_ref, x_ref, out_ref, acc_ref, p_ref,
                  *, eb, heads):
    e = pl.program_id(1)

    @pl.when(e == 0)
    def _():
        acc_ref[...] = jnp.zeros(acc_ref.shape, F32)

    pre = _dot(u_ref[...], hnt_ref[...])
    act = 0.5 * pre * (1.0 + lax.erf(pre * (2.0 ** -0.5)))
    per = eb // NKEYS
    for ii in range(per):
        i = e * per + ii
        w = jnp.zeros((NKEYS, act.shape[1]), F32)
        for h in range(heads):
            kept = rho_ref[h] < n1_ref[h, pl.ds(i, 1), :]
            w = w + jnp.where(kept, e2_ref[h], 0.0) * r1_ref[h, pl.ds(i, 1), :]
        p_ref[ii * NKEYS:(ii + 1) * NKEYS, :] = (w * act[ii * NKEYS:(ii + 1) * NKEYS]).astype(BF16)
    acc_ref[...] += _dot(vt_ref[...], p_ref[...])

    @pl.when(e == pl.num_programs(1) - 1)
    def _():
        out_ref[...] = x_ref[...] + acc_ref[...].T


def _dense(x, hnt, u, vt, rho, e2, n1, r1, tt, eb):
    n, d = x.shape
    n_exp = u.shape[0]
    heads = rho.shape[0]
    tab = pl.BlockSpec((heads, NKEYS, tt), lambda i, e: (0, 0, i))
    return pl.pallas_call(
        functools.partial(_dense_kernel, eb=eb, heads=heads),
        grid=(n // tt, n_exp // eb),
        in_specs=[pl.BlockSpec((d, tt), lambda i, e: (0, i)),
                  pl.BlockSpec((eb, d), lambda i, e: (e, 0)),
                  pl.BlockSpec((d, eb), lambda i, e: (0, e)),
                  tab, tab, tab, tab,
                  pl.BlockSpec((tt, d), lambda i, e: (i, 0))],
        out_specs=pl.BlockSpec((tt, d), lambda i, e: (i, 0)),
        out_shape=jax.ShapeDtypeStruct((n, d), F32),
        scratch_shapes=[pltpu.VMEM((d, tt), F32), pltpu.VMEM((eb, tt), BF16)],
        compiler_params=_params("arbitrary", "arbitrary"),
        name="peer_dense",
    )(hnt, u, vt, rho, e2, n1, r1, x)


def _peer(x, g, wqt, keys, u, vt, tt_route, tt_dense, eb):
    hnt, rho, e2, n1, r1 = _route(x, g, wqt, keys, tt_route)
    return _dense(x, hnt, u, vt, rho, e2, n1, r1, tt_dense, eb)


def _sb_prompt_kernel(bias_ref, q_ref, k_ref, v_ref, tri_ref, o_ref, *, tq):
    hp = pl.program_id(1)
    qi = pl.program_id(2)
    q = q_ref[0] * (HEAD_DIM ** -0.5)
    lane = lax.broadcasted_iota(jnp.int32, (1, LANES), 1)
    strict = (lax.broadcasted_iota(jnp.int32, (tq, tq), 1) < lax.broadcasted_iota(jnp.int32, (tq, tq), 0))
    tri = tri_ref[...]
    out = jnp.zeros((tq, LANES), F32)
    for half in range(LANES // HEAD_DIM):
        mine = (lane // HEAD_DIM) == half
        qh = jnp.where(mine, q, 0.0).astype(BF16)
        bias = bias_ref[hp * (LANES // HEAD_DIM) + half]

        def block(j, carry, o_acc, diag):
            start = pl.multiple_of(j * tq, tq)
            kb = k_ref[0, pl.ds(start, tq), :].astype(BF16)
            vb = v_ref[0, pl.ds(start, tq), :].astype(BF16)
            z = _dot_nt(qh, kb) + bias
            lk = _neg_softplus(z)
            if diag:
                lk = jnp.where(strict, lk, 0.0)
            hi, lo = _split(lk)
            cum = _dot(hi, tri) + _dot(lo, tri)
            a = jnp.exp(z + cum + carry)
            if diag:
                a = jnp.where(strict, a, 0.0)
            return carry + jnp.sum(lk, axis=1, keepdims=True), o_acc + _dot(a.astype(BF16), vb)

        carry, o_acc = block(qi, jnp.zeros((tq, 1), F32), jnp.zeros((tq, LANES), F32), True)
        carry, o_acc = lax.fori_loop(0, qi, lambda jj, c: block(qi - 1 - jj, c[0], c[1], False), (carry, o_acc))
        out = jnp.where(mine, o_acc, out)
    o_ref[0] = out


def _sb_prompt(q, k, v, bias, tri, tq):
    b, t, w = q.shape
    kv = pl.BlockSpec((1, t, LANES), lambda i, hp, qi: (i, 0, hp))
    qo = pl.BlockSpec((1, tq, LANES), lambda i, hp, qi: (i, qi, hp))
    return pl.pallas_call(
        functools.partial(_sb_prompt_kernel, tq=tq),
        grid=(b, w // LANES, t // tq),
        in_specs=[pl.BlockSpec(memory_space=pltpu.SMEM), qo, kv, kv, _full((tq, tq))],
        out_specs=qo,
        out_shape=jax.ShapeDtypeStruct((b, t, w), F32),
        compiler_params=_params("arbitrary", "arbitrary", "arbitrary"),
        name="stick_breaking_prompt",
    )(bias, q, k, v, tri)


SB_ROWS = 8


def _sb_sample_kernel(pt_ref, qmt_ref, bias_ref, kn_ref, vn_ref, kp_ref, vp_ref, trit_ref, bmask_ref, o_ref,
                      carry_ref, acc_ref, *, n_new, heads):
    del pt_ref
    p = pl.program_id(1)
    page = kp_ref.shape[1]

    def process(kb, vb, valid):
        z = _dot(kb, qmt_ref[0]) + bias_ref[...]
        lk = _neg_softplus(z)
        if valid is not None:
            lk = jnp.where(valid, lk, 0.0)
        hi, lo = _split(lk)
        cum = _dot(trit_ref[...], hi) + _dot(trit_ref[...], lo)
        a = jnp.exp(z + cum + carry_ref[...])
        if valid is not None:
            a = jnp.where(valid, a, 0.0)
        acc_ref[...] += _dot(a.T.astype(BF16), vb)
        carry_ref[...] += jnp.sum(lk, axis=0, keepdims=True)

    @pl.when(p == 0)
    def _():
        carry_ref[...] = jnp.zeros(carry_ref.shape, F32)
        acc_ref[...] = jnp.zeros(acc_ref.shape, F32)
        key = lax.broadcasted_iota(jnp.int32, (page, LANES), 0)
        query = lax.broadcasted_iota(jnp.int32, (page, LANES), 1) % SB_ROWS
        process(kn_ref[0].astype(BF16), vn_ref[0].astype(BF16), (key < query) & (key < n_new))

    @pl.when(p > 0)
    def _():
        process(kp_ref[0].astype(BF16), vp_ref[0].astype(BF16), None)

    @pl.when(p == pl.num_programs(1) - 1)
    def _():
        own = acc_ref[...] * bmask_ref[...]
        out = own[0:SB_ROWS]
        for h in range(1, heads):
            out = out + own[h * SB_ROWS:(h + 1) * SB_ROWS]
        o_ref[0] = out


def _sb_sample(page_table, qmt, bias_cols, k_new, v_new, cache_k, cache_v, trit, bmask, n_new, heads):
    n_seq, n_pages = page_table.shape
    _, page, w = cache_k.shape

    def page_map(i, p, pt):
        return (pt[i * n_pages + n_pages - jnp.maximum(p, 1)], 0, 0)

    per_seq = lambda shape: pl.BlockSpec((1,) + shape, lambda i, p, pt: (i, 0, 0))
    const = lambda shape: pl.BlockSpec(shape, lambda i, p, pt: (0, 0))
    grid_spec = pltpu.PrefetchScalarGridSpec(
        num_scalar_prefetch=1,
        grid=(n_seq, n_pages + 1),
        in_specs=[per_seq((w, LANES)), const((1, LANES)), per_seq((page, w)), per_seq((page, w)),
                  pl.BlockSpec((1, page, w), page_map), pl.BlockSpec((1, page, w), page_map),
                  const((page, page)), const((LANES, w))],
        out_specs=per_seq((SB_ROWS, w)),
        scratch_shapes=[pltpu.VMEM((1, LANES), F32), pltpu.VMEM((LANES, w), F32)],
    )
    return pl.pallas_call(
        functools.partial(_sb_sample_kernel, n_new=n_new, heads=heads),
        grid_spec=grid_spec,
        out_shape=jax.ShapeDtypeStruct((n_seq, SB_ROWS, w), F32),
        compiler_params=_params("arbitrary", "arbitrary"),
        name="stick_breaking_sample",
    )(page_table.reshape(-1), qmt, bias_cols, k_new, v_new, cache_k, cache_v, trit, bmask)


def _head_indicator(width):
    head = jnp.arange(width) // HEAD_DIM
    return (head[:, None] == head[None, :]).astype(BF16)


def _tiles(t):
    seq = min(t, 256)
    return seq


def kernel(x_prompt, x_sample, mem_prompt, state_conv, cache_k, cache_v, cache_mem_k, cache_mem_v, page_table,
           g_norm1, g_norm2, g_mem_norm, w_mem_kv, g_mem_q, g_mem_k, w_in_a, w_dw, b_dw, g_conv_ln, b_conv_ln,
           w_out_a, g_kv, w_kv, g_sb_k, w_in_b, g_sb_q, sb_bias, w_out_b, peer_w_q, peer_sub_keys, peer_u,
           peer_v):
    d = x_prompt.shape[-1]
    c_conv = w_dw.shape[-1]
    conv_w = w_dw.shape[1]
    sb_w = w_kv.shape[-1] // 2
    mem_w = w_mem_kv.shape[-1] // 2
    sb_heads = sb_w // HEAD_DIM
    mem_heads = mem_w // HEAD_DIM
    n_mem = mem_prompt.shape[1]
    dec_b, dec_t, _ = x_sample.shape

    row = lambda v: v.reshape(1, -1)
    tile_g = lambda g, heads: jnp.tile(g, heads).reshape(1, -1)
    ind_m = _head_indicator(mem_w)
    ind_sb = _head_indicator(sb_w)
    w_in_a_b = w_in_a[0].astype(BF16)
    w_out_a_b = w_out_a[0].astype(BF16)
    w_in_b_b = w_in_b[0].astype(BF16)
    w_out_b_b = w_out_b[0].astype(BF16)
    w_kv_b = w_kv.astype(BF16)
    wqt = [peer_w_q[l].T.astype(BF16) for l in range(2)]
    keys = [peer_sub_keys[l].astype(BF16) for l in range(2)]
    u_b = [peer_u[l].astype(BF16) for l in range(2)]
    vt_b = [peer_v[l].T.astype(BF16) for l in range(2)]
    tri = (jnp.arange(LANES)[:, None] >= jnp.arange(LANES)[None, :]).astype(BF16)

    b_p, t_p, _ = x_prompt.shape
    mk, mv = _memory_kv(mem_prompt.reshape(b_p * n_mem, d), g_mem_norm.reshape(2, 1, d), w_mem_kv.astype(BF16),
                        jnp.tile(g_mem_k, (1, mem_heads)).reshape(2, 1, mem_w), ind_m)
    mem_k_prompt = mk.reshape(2, b_p, n_mem, mem_heads, HEAD_DIM)
    mem_v_prompt = mv.reshape(2, b_p, n_mem, mem_heads, HEAD_DIM)

    def forward(x, prev, mem_k, mem_v, paged):
        b, t, _ = x.shape
        tt = min(t, 256)
        n = b * t
        tt_route = min(n, 256)
        tt_dense = min(n, 512)
        glu, qm = _proj_a(x, row(g_norm1[0]), w_in_a_b, tile_g(g_mem_q[0], mem_heads), ind_m, c_conv, tt)
        c_out = _conv(glu, prev, w_dw[0], row(b_dw[0]), row(g_conv_ln[0]), row(b_conv_ln[0]), tt)
        x = _mix(x, c_out, qm, mem_k[0], mem_v[0], w_out_a_b[:c_conv], w_out_a_b[c_conv:], tt)
        x = _peer(x.reshape(n, d), row(g_norm2[0]), wqt[0], keys[0], u_b[0], vt_b[0],
                  tt_route, tt_dense, 512).reshape(b, t, d)
        k, v, q, qm = _proj_b(x, row(g_kv), w_kv_b, tile_g(g_sb_k, sb_heads), row(g_norm1[1]), w_in_b_b,
                              tile_g(g_sb_q[0], sb_heads), tile_g(g_mem_q[1], mem_heads), ind_sb, ind_m, sb_w, tt)
        if paged is None:
            o = _sb_prompt(q, k, v, sb_bias[0], tri, LANES)
        else:
            o = paged(q, k, v)
        x = _mix(x, o, qm, mem_k[1], mem_v[1], w_out_b_b[:sb_w], w_out_b_b[sb_w:], tt)
        x = _peer(x.reshape(n, d), row(g_norm2[1]), wqt[1], keys[1], u_b[1], vt_b[1],
                  tt_route, tt_dense, 512).reshape(b, t, d)
        return x, glu, k, v

    prev0 = jnp.zeros((b_p, CONV_HALO, c_conv), F32)
    y_prompt, glu_p, k_p, v_p = forward(x_prompt, prev0, mk.reshape(2, b_p, n_mem, mem_w),
                                        mv.reshape(2, b_p, n_mem, mem_w), None)
    conv_prompt = glu_p[None, :, t_p - (conv_w - 1):, :]

    n_pool, page, _, _ = cache_k.shape
    pad_t = SB_ROWS - dec_t
    x_s = jnp.pad(x_sample, ((0, 0), (0, pad_t), (0, 0)))
    prev_s = jnp.pad(state_conv[0], ((0, 0), (CONV_HALO - (conv_w - 1), 0), (0, 0)))
    col_head = jnp.arange(LANES) // SB_ROWS
    col_query = jnp.arange(LANES) % SB_ROWS
    lane_head = jnp.arange(sb_w) // HEAD_DIM
    col_ok = (col_head < sb_heads) & (col_query < dec_t)
    place = ((lane_head[:, None] == col_head[None, :]) & col_ok[None, :]).astype(F32)
    bias_cols = jnp.where(col_head < sb_heads, sb_bias[0][jnp.minimum(col_head, sb_heads - 1)], 0.0).reshape(1, LANES)
    bmask = place.T
    trit = tri.T

    def paged(q, k, v):
        q_cols = jnp.take(q * (HEAD_DIM ** -0.5), jnp.minimum(col_query, SB_ROWS - 1), axis=1)
        qmt = (jnp.swapaxes(q_cols, 1, 2) * place[None]).astype(BF16)
        grow = lambda a: jnp.pad(a[:, :dec_t], ((0, 0), (0, page - dec_t), (0, 0)))
        return _sb_sample(page_table, qmt, bias_cols, grow(k), grow(v), cache_k.reshape(n_pool, page, sb_w),
                          cache_v.reshape(n_pool, page, sb_w), trit, bmask, dec_t, sb_heads)

    y_s, glu_s, k_s, v_s = forward(x_s, prev_s, cache_mem_k.reshape(2, dec_b, n_mem, mem_w),
                                   cache_mem_v.reshape(2, dec_b, n_mem, mem_w), paged)
    y_sample = y_s[:, :dec_t]
    conv_sample = jnp.concatenate([state_conv[0][:, dec_t:], glu_s[:, :dec_t]], axis=1)[None]
    shape_kv = lambda a: a.reshape(a.shape[0], a.shape[1], sb_heads, HEAD_DIM)
    return (y_prompt, y_sample, conv_prompt, conv_sample, shape_kv(k_p), shape_kv(v_p),
            shape_kv(k_s[:, :dec_t]), shape_kv(v_s[:, :dec_t]), mem_k_prompt, mem_v_prompt)
```

```python
import functools

import jax
import jax.numpy as jnp
from jax import lax
from jax.experimental import pallas as pl
from jax.experimental.pallas import tpu as pltpu

F32 = jnp.float32
BF16 = jnp.bfloat16

HEAD_DIM = 64
EPS = 1e-6
TOPK = 16
NKEYS = 128
LANES = 128
VMEM_LIMIT = 56 * 1024 * 1024
NEG_INF = float("-inf")


def _params(*sem, **kw):
    return pltpu.CompilerParams(dimension_semantics=sem, vmem_limit_bytes=VMEM_LIMIT, **kw)


def _dot(a, b):
    return jnp.dot(a, b, preferred_element_type=F32)


def _dot_nt(a, b):
    return lax.dot_general(a, b, (((1,), (1,)), ((), ())), preferred_element_type=F32)


def _split(x):
    hi = x.astype(BF16)
    return hi, (x - hi.astype(F32)).astype(BF16)


def _rms(x, g):
    return x * lax.rsqrt(jnp.mean(x * x, axis=-1, keepdims=True) + EPS) * g


def _head_rms(x, g, ind):
    hi, lo = _split(x * x)
    ms = (_dot(hi, ind) + _dot(lo, ind)) * (1.0 / HEAD_DIM)
    return x * lax.rsqrt(ms + EPS) * g


def _neg_softplus(z):
    return -(jnp.maximum(z, 0.0) + jnp.log(1.0 + jnp.exp(-jnp.abs(z))))


def _full(shape):
    n = len(shape)
    return pl.BlockSpec(shape, lambda *_: (0,) * n)


def _memkv_kernel(mem_ref, g_ref, w_ref, gk_ref, ind_ref, k_ref, v_ref, *, width):
    h = _rms(mem_ref[...], g_ref[0]).astype(BF16)
    kv = _dot(h, w_ref[0])
    k_ref[0] = _head_rms(kv[:, :width], gk_ref[0], ind_ref[...])
    v_ref[0] = kv[:, width:]


def _memory_kv(mem, g_norm, w_kv, g_k, ind):
    depth, d, w2 = w_kv.shape
    width = w2 // 2
    n = mem.shape[0]
    return pl.pallas_call(
        functools.partial(_memkv_kernel, width=width),
        grid=(depth,),
        in_specs=[_full((n, d)),
                  pl.BlockSpec((1, 1, d), lambda l: (l, 0, 0)),
                  pl.BlockSpec((1, d, w2), lambda l: (l, 0, 0)),
                  pl.BlockSpec((1, 1, width), lambda l: (l, 0, 0)),
                  _full((width, width))],
        out_specs=[pl.BlockSpec((1, n, width), lambda l: (l, 0, 0))] * 2,
        out_shape=[jax.ShapeDtypeStruct((depth, n, width), F32)] * 2,
        compiler_params=_params("arbitrary"),
        name="memory_kv",
    )(mem, g_norm, w_kv, g_k, ind)


def _proj_a_kernel(x_ref, g_ref, w_ref, gq_ref, ind_ref, glu_ref, qm_ref, *, c):
    h = _rms(x_ref[0], g_ref[...]).astype(BF16)
    p = _dot(h, w_ref[...])
    glu_ref[0] = p[:, :c] * jax.nn.sigmoid(p[:, c:2 * c])
    qm_ref[0] = _head_rms(p[:, 2 * c:], gq_ref[...], ind_ref[...])


def _proj_a(x, g, w, gq, ind, c, tt):
    b, t, d = x.shape
    wm = w.shape[1] - 2 * c
    tile = lambda width: pl.BlockSpec((1, tt, width), lambda i, j: (i, j, 0))
    return pl.pallas_call(
        functools.partial(_proj_a_kernel, c=c),
        grid=(b, t // tt),
        in_specs=[tile(d), _full((1, d)), _full(w.shape), _full((1, wm)), _full((wm, wm))],
        out_specs=[tile(c), tile(wm)],
        out_shape=[jax.ShapeDtypeStruct((b, t, c), F32), jax.ShapeDtypeStruct((b, t, wm), F32)],
        compiler_params=_params("arbitrary", "arbitrary"),
        name="in_proj_a",
    )(x, g, w, gq, ind)


def _proj_b_kernel(x_ref, gkv_ref, wkv_ref, gk_ref, g1_ref, wb_ref, gq_ref, gm_ref, ind_sb_ref, ind_m_ref,
                   k_ref, v_ref, q_ref, qm_ref, *, sb):
    x = x_ref[0]
    kv = _dot(_rms(x, gkv_ref[...]).astype(BF16), wkv_ref[...])
    k_ref[0] = _head_rms(kv[:, :sb], gk_ref[...], ind_sb_ref[...])
    v_ref[0] = kv[:, sb:]
    p = _dot(_rms(x, g1_ref[...]).astype(BF16), wb_ref[...])
    q_ref[0] = _head_rms(p[:, :sb], gq_ref[...], ind_sb_ref[...])
    qm_ref[0] = _head_rms(p[:, sb:], gm_ref[...], ind_m_ref[...])


def _proj_b(x, gkv, wkv, gk, g1, wb, gq, gm, ind_sb, ind_m, sb, tt):
    b, t, d = x.shape
    wm = wb.shape[1] - sb
    tile = lambda width: pl.BlockSpec((1, tt, width), lambda i, j: (i, j, 0))
    sds = lambda width: jax.ShapeDtypeStruct((b, t, width), F32)
    return pl.pallas_call(
        functools.partial(_proj_b_kernel, sb=sb),
        grid=(b, t // tt),
        in_specs=[tile(d), _full((1, d)), _full(wkv.shape), _full((1, sb)), _full((1, d)), _full(wb.shape),
                  _full((1, sb)), _full((1, wm)), _full((sb, sb)), _full((wm, wm))],
        out_specs=[tile(sb), tile(sb), tile(sb), tile(wm)],
        out_shape=[sds(sb), sds(sb), sds(sb), sds(wm)],
        compiler_params=_params("arbitrary", "arbitrary"),
        name="in_proj_b",
    )(x, gkv, wkv, gk, g1, wb, gq, gm, ind_sb, ind_m)


CONV_HALO = 32
CONV_ROWS = 64


def _conv_kernel(glu_ref, prev_ref, w_ref, b_ref, g_ref, bl_ref, out_ref, pad_ref, y_ref, *, tt, width, carry):
    c = glu_ref.shape[-1]

    @pl.when(pl.program_id(1) == 0)
    def _():
        pad_ref[0:CONV_HALO, :] = prev_ref[0]

    pad_ref[CONV_HALO:CONV_HALO + tt, :] = glu_ref[0]
    first = CONV_HALO - (width - 1)
    rows = min(CONV_ROWS, tt)
    for r0 in range(0, tt, rows):
        for l0 in range(0, c, LANES):
            acc = jnp.zeros((rows, LANES), F32) + b_ref[:, l0:l0 + LANES]
            for k in range(width):
                acc = acc + w_ref[k:k + 1, l0:l0 + LANES] * pad_ref[first + k + r0:first + k + r0 + rows,
                                                                    l0:l0 + LANES]
            y_ref[r0:r0 + rows, l0:l0 + LANES] = acc
    y = y_ref[...]
    mu = jnp.mean(y, axis=-1, keepdims=True)
    var = jnp.mean(jnp.square(y - mu), axis=-1, keepdims=True)
    yn = (y - mu) * lax.rsqrt(var + EPS) * g_ref[...] + bl_ref[...]
    out_ref[0] = yn * jax.nn.sigmoid(yn)
    if carry:
        pad_ref[0:CONV_HALO, :] = pad_ref[tt:tt + CONV_HALO, :]


def _conv(glu, prev, w, bias, g_ln, b_ln, tt):
    b, t, c = glu.shape
    width = w.shape[0]
    tile = pl.BlockSpec((1, tt, c), lambda i, j: (i, j, 0))
    return pl.pallas_call(
        functools.partial(_conv_kernel, tt=tt, width=width, carry=t > tt),
        grid=(b, t // tt),
        in_specs=[tile, pl.BlockSpec((1, CONV_HALO, c), lambda i, j: (i, 0, 0)), _full(w.shape),
                  _full((1, c)), _full((1, c)), _full((1, c))],
        out_specs=tile,
        out_shape=jax.ShapeDtypeStruct((b, t, c), F32),
        scratch_shapes=[pltpu.VMEM((CONV_HALO + tt, c), F32), pltpu.VMEM((tt, c), F32)],
        compiler_params=_params("arbitrary", "arbitrary"),
        name="conformer_conv",
    )(glu, prev, w, bias, g_ln, b_ln)


def _mix_kernel(x_ref, c_ref, qm_ref, mk_ref, mv_ref, wc_ref, wm_ref, out_ref, *, heads):
    qm = qm_ref[0]
    mk = mk_ref[0].astype(BF16)
    mv = mv_ref[0].astype(BF16)
    head_of_lane = lax.broadcasted_iota(jnp.int32, (1, qm.shape[-1]), 1) // HEAD_DIM
    m_out = jnp.zeros(qm.shape, F32)
    for h in range(heads):
        mine = head_of_lane == h
        s = _dot_nt(jnp.where(mine, qm, 0.0).astype(BF16), mk) * (HEAD_DIM ** -0.5)
        e = jnp.exp(s - jnp.max(s, axis=-1, keepdims=True))
        p = e / jnp.sum(e, axis=-1, keepdims=True)
        m_out = jnp.where(mine, _dot(p.astype(BF16), mv), m_out)
    out_ref[0] = (x_ref[0] + _dot(c_ref[0].astype(BF16), wc_ref[...])
                  + _dot(m_out.astype(BF16), wm_ref[...]))


def _mix(x, c_out, qm, mem_k, mem_v, wc, wm, tt):
    b, t, d = x.shape
    cw, mw = c_out.shape[-1], qm.shape[-1]
    n_mem = mem_k.shape[1]
    tile = lambda width: pl.BlockSpec((1, tt, width), lambda i, j: (i, j, 0))
    mem = pl.BlockSpec((1, n_mem, mw), lambda i, j: (i, 0, 0))
    return pl.pallas_call(
        functools.partial(_mix_kernel, heads=mw // HEAD_DIM),
        grid=(b, t // tt),
        in_specs=[tile(d), tile(cw), tile(mw), mem, mem, _full(wc.shape), _full(wm.shape)],
        out_specs=tile(d),
        out_shape=jax.ShapeDtypeStruct((b, t, d), F32),
        compiler_params=_params("arbitrary", "arbitrary"),
        name="mem_attn_out_proj",
    )(x, c_out, qm, mem_k, mem_v, wc, wm)


def _top16(s, key_idx):
    n = s.shape[0]
    work = s
    rank = jnp.full(s.shape, float(TOPK), F32)
    vals = []
    for r in range(TOPK):
        m = jnp.max(work, axis=0, keepdims=True)
        first = jnp.min(jnp.where(work == m, key_idx, float(n)), axis=0, keepdims=True)
        sel = key_idx == first
        rank = jnp.where(sel, float(r), rank)
        work = jnp.where(sel, NEG_INF, work)
        vals.append(m)
    return vals, rank


def _pair_top16(sv1, sv2, cols):
    sub = 8
    row = lax.broadcasted_iota(jnp.int32, (TOPK, cols), 0).astype(F32)
    sv2_col = jnp.zeros((TOPK, cols), F32)
    for c in range(TOPK):
        sv2_col = jnp.where(row == float(c), sv2[c], sv2_col)
    cands, flat = [], []
    for r in range(TOPK):
        nrow = TOPK if r == 0 else sub
        cands.append(jnp.where(row[:nrow] < float(TOPK // (r + 1)), sv1[r] + sv2_col[:nrow], NEG_INF))
        flat.append(row[:nrow] + float(TOPK * r))
    cand = jnp.concatenate(cands, axis=0)
    flat = jnp.concatenate(flat, axis=0)
    top = sv1[0] + sv2[0]
    kept = jnp.zeros(cand.shape, F32)
    z = jnp.zeros((1, cols), F32)
    for _ in range(TOPK):
        m = jnp.max(cand, axis=0, keepdims=True)
        first = jnp.min(jnp.where(cand == m, flat, float(TOPK * TOPK)), axis=0, keepdims=True)
        sel = flat == first
        kept = jnp.where(sel, 1.0, kept)
        cand = jnp.where(sel, NEG_INF, cand)
        z = z + jnp.exp(m - top)
    counts, o = [], 0
    for r in range(TOPK):
        nrow = TOPK if r == 0 else sub
        counts.append(jnp.sum(kept[o:o + nrow], axis=0, keepdims=True))
        o += nrow
    return counts, z


def _route_kernel(x_ref, g_ref, wqt_ref, keys_ref, hnt_ref, rho_ref, e2_ref, n1_ref, r1_ref, qt_ref, *, tt, heads):
    hnt = _rms(x_ref[...], g_ref[...]).T.astype(BF16)
    hnt_ref[...] = hnt
    qt_ref[...] = _dot(wqt_ref[...], hnt)
    key_idx = lax.broadcasted_iota(jnp.int32, (NKEYS, LANES), 0).astype(F32)

    def head(h, _):
        base = pl.multiple_of(h * 2 * NKEYS, 2 * NKEYS)
        for l0 in range(0, tt, LANES):
            ls = slice(l0, l0 + LANES)
            s1 = _dot(keys_ref[h, 0], qt_ref[pl.ds(base, NKEYS), ls].astype(BF16))
            s2 = _dot(keys_ref[h, 1], qt_ref[pl.ds(base + NKEYS, NKEYS), ls].astype(BF16))
            sv1, rank1 = _top16(s1, key_idx)
            sv2, rank2 = _top16(s2, key_idx)
            counts, z = _pair_top16(sv1, sv2, LANES)
            n1 = jnp.zeros(s1.shape, F32)
            for r in range(TOPK):
                n1 = jnp.where(rank1 == float(r), counts[r], n1)
            rho_ref[h, :, ls] = rank2.astype(BF16)
            n1_ref[h, :, ls] = n1
            e2_ref[h, :, ls] = (jnp.exp(s2 - sv2[0]) / z).astype(BF16)
            r1_ref[h, :, ls] = jnp.exp(s1 - sv1[0])
        return 0

    lax.fori_loop(0, heads, head, 0)


def _route(x, g, wqt, keys, tt):
    n, d = x.shape
    heads = keys.shape[0]
    tab = pl.BlockSpec((heads, NKEYS, tt), lambda i: (0, 0, i))
    tab_shape = lambda dtype: jax.ShapeDtypeStruct((heads, NKEYS, n), dtype)
    return pl.pallas_call(
        functools.partial(_route_kernel, tt=tt, heads=heads),
        grid=(n // tt,),
        in_specs=[pl.BlockSpec((tt, d), lambda i: (i, 0)), _full((1, d)), _full(wqt.shape), _full(keys.shape)],
        out_specs=[pl.BlockSpec((d, tt), lambda i: (0, i)), tab, tab, tab, tab],
        out_shape=[jax.ShapeDtypeStruct((d, n), BF16), tab_shape(BF16), tab_shape(BF16), tab_shape(F32),
                   tab_shape(F32)],
        scratch_shapes=[pltpu.VMEM((wqt.shape[0], tt), F32)],
        compiler_params=_params("arbitrary"),
        name="peer_route",
    )(x, g, wqt, keys)


def _dense_kernel(hnt_ref, u_ref, v_ref, rho_ref, e2_ref, n1_ref, r1_ref, x_ref, out_ref, acc_ref, pre_ref,
                  p_ref, *, eb, heads):
    g = pl.program_id(1)
    n_blocks = 2 * (pl.num_programs(1) - 1)
    tt = acc_ref.shape[1]
    per = eb // NKEYS

    @pl.when(g == 0)
    def _():
        acc_ref[...] = jnp.zeros(acc_ref.shape, F32)
        pre_ref[...] = jnp.zeros(pre_ref.shape, F32)
        p_ref[...] = jnp.zeros(p_ref.shape, BF16)

    for slot in range(2):
        s = 2 * g + slot
        rows = slice(slot * eb, (slot + 1) * eb)
        acc_ref[...] += lax.dot_general(v_ref[rows, :], p_ref[slot], (((0,), (0,)), ((), ())),
                                        preferred_element_type=F32)
        blk = jnp.clip(s - 1, 0, n_blocks - 1)
        for ii in range(per):
            i = blk * per + ii
            w = jnp.zeros((NKEYS, tt), BF16)
            for h in range(heads):
                kept = rho_ref[h] < n1_ref[h, pl.ds(i, 1), :].astype(BF16)
                w = w + jnp.where(kept, e2_ref[h], 0) * r1_ref[h, pl.ds(i, 1), :].astype(BF16)
            pre = pre_ref[1 - slot, ii * NKEYS:(ii + 1) * NKEYS, :]
            act = 0.5 * pre * (1.0 + lax.erf(pre * (2.0 ** -0.5)))
            p_ref[1 - slot, ii * NKEYS:(ii + 1) * NKEYS, :] = w * act.astype(BF16)
        pre_ref[slot] = _dot(u_ref[rows, :], hnt_ref[...])

    @pl.when(g == pl.num_programs(1) - 1)
    def _():
        out_ref[...] = x_ref[...] + acc_ref[...].T


def _dense(x, hnt, u, v, rho, e2, n1, r1, tt, eb):
    n, d = x.shape
    n_exp = u.shape[0]
    heads = rho.shape[0]
    pairs = n_exp // (2 * eb)
    tab = pl.BlockSpec((heads, NKEYS, tt), lambda i, s: (0, 0, i))
    return pl.pallas_call(
        functools.partial(_dense_kernel, eb=eb, heads=heads),
        grid=(n // tt, pairs + 1),
        in_specs=[pl.BlockSpec((d, tt), lambda i, s: (0, i)),
                  pl.BlockSpec((2 * eb, d), lambda i, s: (jnp.minimum(s, pairs - 1), 0)),
                  pl.BlockSpec((2 * eb, d), lambda i, s: (jnp.clip(s - 1, 0, pairs - 1), 0)),
                  tab, tab, tab, tab,
                  pl.BlockSpec((tt, d), lambda i, s: (i, 0))],
        out_specs=pl.BlockSpec((tt, d), lambda i, s: (i, 0)),
        out_shape=jax.ShapeDtypeStruct((n, d), F32),
        scratch_shapes=[pltpu.VMEM((d, tt), F32), pltpu.VMEM((2, eb, tt), F32), pltpu.VMEM((2, eb, tt), BF16)],
        compiler_params=_params("arbitrary", "arbitrary"),
        name="peer_dense",
    )(hnt, u, v, rho, e2, n1, r1, x)


def _peer(x, g, wqt, keys, u, v, tt_route, tt_dense, eb):
    hnt, rho, e2, n1, r1 = _route(x, g, wqt, keys, tt_route)
    return _dense(x, hnt, u, v, rho, e2, n1, r1, tt_dense, eb)


def _sb_prompt_kernel(bias_ref, q_ref, k_ref, v_ref, tri_ref, o_ref, *, tq):
    hp = pl.program_id(1)
    qi = pl.program_id(2)
    q = q_ref[0] * (HEAD_DIM ** -0.5)
    first_head = lax.broadcasted_iota(jnp.int32, (1, LANES), 1) < HEAD_DIM
    q2 = jnp.concatenate([jnp.where(first_head, q, 0.0), jnp.where(first_head, 0.0, q)], axis=0).astype(BF16)
    bias0 = bias_ref[hp * 2]
    bias1 = bias_ref[hp * 2 + 1]
    strict = (lax.broadcasted_iota(jnp.int32, (tq, tq), 1) < lax.broadcasted_iota(jnp.int32, (tq, tq), 0))
    strict = jnp.concatenate([strict, strict], axis=0)
    tri = tri_ref[...]

    def chunk(c, carry, o_acc, diag):
        start = pl.multiple_of(c * tq, tq)
        kb = k_ref[0, pl.ds(start, tq), :].astype(BF16)
        vb = v_ref[0, pl.ds(start, tq), :].astype(BF16)
        z = _dot_nt(q2, kb)
        z = jnp.concatenate([z[:tq] + bias0, z[tq:] + bias1], axis=0)
        lk = _neg_softplus(z)
        if diag:
            lk = jnp.where(strict, lk, 0.0)
        hi, lo = _split(lk)
        parts = []
        for s in reversed(range(tq // LANES)):
            sl = slice(s * LANES, (s + 1) * LANES)
            cum = _dot(hi[:, sl], tri) + _dot(lo[:, sl], tri)
            a = jnp.exp(z[:, sl] + cum + carry)
            if diag:
                a = jnp.where(strict[:, sl], a, 0.0)
            parts.insert(0, a.astype(BF16))
            carry = carry + jnp.sum(lk[:, sl], axis=1, keepdims=True)
        return carry, o_acc + _dot(jnp.concatenate(parts, axis=1), vb)

    carry, o_acc = chunk(qi, jnp.zeros((2 * tq, 1), F32), jnp.zeros((2 * tq, LANES), F32), True)
    carry, o_acc = lax.fori_loop(0, qi, lambda jj, c: chunk(qi - 1 - jj, c[0], c[1], False), (carry, o_acc))
    o_ref[0] = jnp.where(first_head, o_acc[:tq], o_acc[tq:])


def _sb_prompt(q, k, v, bias, tri, tq):
    b, t, w = q.shape
    kv = pl.BlockSpec((1, t, LANES), lambda i, hp, qi: (i, 0, hp))
    qo = pl.BlockSpec((1, tq, LANES), lambda i, hp, qi: (i, qi, hp))
    return pl.pallas_call(
        functools.partial(_sb_prompt_kernel, tq=tq),
        grid=(b, w // LANES, t // tq),
        in_specs=[pl.BlockSpec(memory_space=pltpu.SMEM), qo, kv, kv, _full(tri.shape)],
        out_specs=qo,
        out_shape=jax.ShapeDtypeStruct((b, t, w), F32),
        compiler_params=_params("arbitrary", "arbitrary", "arbitrary"),
        name="stick_breaking_prompt",
    )(bias, q, k, v, tri)


SB_ROWS = 8


SB_PAGES = 4


def _sb_sample_kernel(pt_ref, qmt_ref, bias_ref, kn_ref, vn_ref, *rest, n_new, heads):
    del pt_ref
    kp_refs, vp_refs = rest[:SB_PAGES], rest[SB_PAGES:2 * SB_PAGES]
    trit_ref, bmask_ref, o_ref, carry_ref, acc_ref = rest[2 * SB_PAGES:]
    p = pl.program_id(1)
    page = kn_ref.shape[1]

    def process(k_pages, v_pages, valid):
        kb = jnp.concatenate(k_pages, axis=0).astype(BF16)
        vb = jnp.concatenate(v_pages, axis=0).astype(BF16)
        z = _dot(kb, qmt_ref[0]) + bias_ref[...]
        lk = _neg_softplus(z)
        if valid is not None:
            lk = jnp.where(valid, lk, 0.0)
        hi, lo = _split(lk)
        carry = carry_ref[...]
        parts = []
        for s in reversed(range(len(k_pages))):
            sl = slice(s * page, (s + 1) * page)
            cum = _dot(trit_ref[...], hi[sl]) + _dot(trit_ref[...], lo[sl])
            parts.insert(0, jnp.exp(z[sl] + cum + carry))
            carry = carry + jnp.sum(lk[sl], axis=0, keepdims=True)
        a = jnp.concatenate(parts, axis=0)
        if valid is not None:
            a = jnp.where(valid, a, 0.0)
        acc_ref[...] += _dot(a.T.astype(BF16), vb)
        carry_ref[...] = carry

    @pl.when(p == 0)
    def _():
        carry_ref[...] = jnp.zeros(carry_ref.shape, F32)
        acc_ref[...] = jnp.zeros(acc_ref.shape, F32)
        key = lax.broadcasted_iota(jnp.int32, (page, LANES), 0)
        query = lax.broadcasted_iota(jnp.int32, (page, LANES), 1) % SB_ROWS
        process([kn_ref[0]], [vn_ref[0]], (key < query) & (key < n_new))

    @pl.when(p > 0)
    def _():
        process([r[0] for r in kp_refs], [r[0] for r in vp_refs], None)

    @pl.when(p == pl.num_programs(1) - 1)
    def _():
        own = acc_ref[...] * bmask_ref[...]
        out = own[0:SB_ROWS]
        for h in range(1, heads):
            out = out + own[h * SB_ROWS:(h + 1) * SB_ROWS]
        o_ref[0] = out


def _sb_sample(page_table, qmt, bias_cols, k_new, v_new, cache_k, cache_v, trit, bmask, n_new, heads):
    n_seq, n_pages = page_table.shape
    _, page, w = cache_k.shape

    def page_spec(s):
        def index_map(i, p, pt):
            return (pt[i * n_pages + n_pages - jnp.maximum(p, 1) * SB_PAGES + s], 0, 0)
        return pl.BlockSpec((1, page, w), index_map)

    per_seq = lambda shape: pl.BlockSpec((1,) + shape, lambda i, p, pt: (i, 0, 0))
    const = lambda shape: pl.BlockSpec(shape, lambda i, p, pt: (0, 0))
    pages = [page_spec(s) for s in range(SB_PAGES)]
    grid_spec = pltpu.PrefetchScalarGridSpec(
        num_scalar_prefetch=1,
        grid=(n_seq, n_pages // SB_PAGES + 1),
        in_specs=[per_seq((w, LANES)), const((1, LANES)), per_seq((page, w)), per_seq((page, w))]
                 + pages + pages + [const((page, page)), const((LANES, w))],
        out_specs=per_seq((SB_ROWS, w)),
        scratch_shapes=[pltpu.VMEM((1, LANES), F32), pltpu.VMEM((LANES, w), F32)],
    )
    return pl.pallas_call(
        functools.partial(_sb_sample_kernel, n_new=n_new, heads=heads),
        grid_spec=grid_spec,
        out_shape=jax.ShapeDtypeStruct((n_seq, SB_ROWS, w), F32),
        compiler_params=_params("arbitrary", "arbitrary"),
        name="stick_breaking_sample",
    )(page_table.reshape(-1), qmt, bias_cols, k_new, v_new, *([cache_k] * SB_PAGES), *([cache_v] * SB_PAGES),
      trit, bmask)


def _head_indicator(width):
    head = jnp.arange(width) // HEAD_DIM
    return (head[:, None] == head[None, :]).astype(BF16)


def _tiles(t):
    seq = min(t, 256)
    return seq


def kernel(x_prompt, x_sample, mem_prompt, state_conv, cache_k, cache_v, cache_mem_k, cache_mem_v, page_table,
           g_norm1, g_norm2, g_mem_norm, w_mem_kv, g_mem_q, g_mem_k, w_in_a, w_dw, b_dw, g_conv_ln, b_conv_ln,
           w_out_a, g_kv, w_kv, g_sb_k, w_in_b, g_sb_q, sb_bias, w_out_b, peer_w_q, peer_sub_keys, peer_u,
           peer_v):
    d = x_prompt.shape[-1]
    c_conv = w_dw.shape[-1]
    conv_w = w_dw.shape[1]
    sb_w = w_kv.shape[-1] // 2
    mem_w = w_mem_kv.shape[-1] // 2
    sb_heads = sb_w // HEAD_DIM
    mem_heads = mem_w // HEAD_DIM
    n_mem = mem_prompt.shape[1]
    dec_b, dec_t, _ = x_sample.shape

    row = lambda v: v.reshape(1, -1)
    tile_g = lambda g, heads: jnp.tile(g, heads).reshape(1, -1)
    ind_m = _head_indicator(mem_w)
    ind_sb = _head_indicator(sb_w)
    w_in_a_b = w_in_a[0].astype(BF16)
    w_out_a_b = w_out_a[0].astype(BF16)
    w_in_b_b = w_in_b[0].astype(BF16)
    w_out_b_b = w_out_b[0].astype(BF16)
    w_kv_b = w_kv.astype(BF16)
    wqt = [peer_w_q[l].T.astype(BF16) for l in range(2)]
    keys = [peer_sub_keys[l].astype(BF16) for l in range(2)]
    u_b = [peer_u[l].astype(BF16) for l in range(2)]
    vt_b = [peer_v[l].astype(BF16) for l in range(2)]
    tri = (jnp.arange(LANES)[:, None] >= jnp.arange(LANES)[None, :]).astype(BF16)

    b_p, t_p, _ = x_prompt.shape
    mk, mv = _memory_kv(mem_prompt.reshape(b_p * n_mem, d), g_mem_norm.reshape(2, 1, d), w_mem_kv.astype(BF16),
                        jnp.tile(g_mem_k, (1, mem_heads)).reshape(2, 1, mem_w), ind_m)
    mem_k_prompt = mk.reshape(2, b_p, n_mem, mem_heads, HEAD_DIM)
    mem_v_prompt = mv.reshape(2, b_p, n_mem, mem_heads, HEAD_DIM)

    def forward(x, prev, mem_k, mem_v, paged):
        b, t, _ = x.shape
        tt = min(t, 256)
        n = b * t
        tt_route = min(n, 256)
        tt_dense = min(n, 512)
        glu, qm = _proj_a(x, row(g_norm1[0]), w_in_a_b, tile_g(g_mem_q[0], mem_heads), ind_m, c_conv, tt)
        c_out = _conv(glu, prev, w_dw[0], row(b_dw[0]), row(g_conv_ln[0]), row(b_conv_ln[0]), tt)
        x = _mix(x, c_out, qm, mem_k[0], mem_v[0], w_out_a_b[:c_conv], w_out_a_b[c_conv:], tt)
        x = _peer(x.reshape(n, d), row(g_norm2[0]), wqt[0], keys[0], u_b[0], vt_b[0],
                  tt_route, tt_dense, 512).reshape(b, t, d)
        k, v, q, qm = _proj_b(x, row(g_kv), w_kv_b, tile_g(g_sb_k, sb_heads), row(g_norm1[1]), w_in_b_b,
                              tile_g(g_sb_q[0], sb_heads), tile_g(g_mem_q[1], mem_heads), ind_sb, ind_m, sb_w, tt)
        if paged is None:
            o = _sb_prompt(q, k, v, sb_bias[0], tri, min(t, 2 * LANES))
        else:
            o = paged(q, k, v)
        x = _mix(x, o, qm, mem_k[1], mem_v[1], w_out_b_b[:sb_w], w_out_b_b[sb_w:], tt)
        x = _peer(x.reshape(n, d), row(g_norm2[1]), wqt[1], keys[1], u_b[1], vt_b[1],
                  tt_route, tt_dense, 512).reshape(b, t, d)
        return x, glu, k, v

    prev0 = jnp.zeros((b_p, CONV_HALO, c_conv), F32)
    y_prompt, glu_p, k_p, v_p = forward(x_prompt, prev0, mk.reshape(2, b_p, n_mem, mem_w),
                                        mv.reshape(2, b_p, n_mem, mem_w), None)
    conv_prompt = glu_p[None, :, t_p - (conv_w - 1):, :]

    n_pool, page, _, _ = cache_k.shape
    pad_t = SB_ROWS - dec_t
    x_s = jnp.pad(x_sample, ((0, 0), (0, pad_t), (0, 0)))
    prev_s = jnp.pad(state_conv[0], ((0, 0), (CONV_HALO - (conv_w - 1), 0), (0, 0)))
    col_head = jnp.arange(LANES) // SB_ROWS
    col_query = jnp.arange(LANES) % SB_ROWS
    lane_head = jnp.arange(sb_w) // HEAD_DIM
    col_ok = (col_head < sb_heads) & (col_query < dec_t)
    place = ((lane_head[:, None] == col_head[None, :]) & col_ok[None, :]).astype(F32)
    bias_cols = jnp.where(col_head < sb_heads, sb_bias[0][jnp.minimum(col_head, sb_heads - 1)], 0.0).reshape(1, LANES)
    bmask = place.T
    trit = tri.T

    def paged(q, k, v):
        q_cols = jnp.take(q * (HEAD_DIM ** -0.5), jnp.minimum(col_query, SB_ROWS - 1), axis=1)
        qmt = (jnp.swapaxes(q_cols, 1, 2) * place[None]).astype(BF16)
        grow = lambda a: jnp.pad(a[:, :dec_t], ((0, 0), (0, page - dec_t), (0, 0)))
        return _sb_sample(page_table, qmt, bias_cols, grow(k), grow(v), cache_k.reshape(n_pool, page, sb_w),
                          cache_v.reshape(n_pool, page, sb_w), trit, bmask, dec_t, sb_heads)

    y_s, glu_s, k_s, v_s = forward(x_s, prev_s, cache_mem_k.reshape(2, dec_b, n_mem, mem_w),
                                   cache_mem_v.reshape(2, dec_b, n_mem, mem_w), paged)
    y_sample = y_s[:, :dec_t]
    conv_sample = jnp.concatenate([state_conv[0][:, dec_t:], glu_s[:, :dec_t]], axis=1)[None]
    shape_kv = lambda a: a.reshape(a.shape[0], a.shape[1], sb_heads, HEAD_DIM)
    return (y_prompt, y_sample, conv_prompt, conv_sample, shape_kv(k_p), shape_kv(v_p),
            shape_kv(k_s[:, :dec_t]), shape_kv(v_s[:, :dec_t]), mem_k_prompt, mem_v_prompt)
```

```python
import functools

import jax
import jax.numpy as jnp
from jax import lax
from jax.experimental import pallas as pl
from jax.experimental.pallas import tpu as pltpu

F32 = jnp.float32
BF16 = jnp.bfloat16

HEAD_DIM = 64
EPS = 1e-6
TOPK = 16
NKEYS = 128
LANES = 128
VMEM_LIMIT = 56 * 1024 * 1024
NEG_INF = float("-inf")


def _params(*sem, **kw):
    return pltpu.CompilerParams(dimension_semantics=sem, vmem_limit_bytes=VMEM_LIMIT, **kw)


def _dot(a, b):
    return jnp.dot(a, b, preferred_element_type=F32)


def _dot_nt(a, b):
    return lax.dot_general(a, b, (((1,), (1,)), ((), ())), preferred_element_type=F32)


def _split(x):
    hi = x.astype(BF16)
    return hi, (x - hi.astype(F32)).astype(BF16)


def _rms(x, g):
    return x * lax.rsqrt(jnp.mean(x * x, axis=-1, keepdims=True) + EPS) * g


def _head_rms(x, g, ind):
    hi, lo = _split(x * x)
    ms = (_dot(hi, ind) + _dot(lo, ind)) * (1.0 / HEAD_DIM)
    return x * lax.rsqrt(ms + EPS) * g


def _neg_softplus(z):
    return -(jnp.maximum(z, 0.0) + jnp.log(1.0 + jnp.exp(-jnp.abs(z))))


def _full(shape):
    n = len(shape)
    return pl.BlockSpec(shape, lambda *_: (0,) * n)


def _memkv_kernel(mem_ref, g_ref, w_ref, gk_ref, ind_ref, k_ref, v_ref, *, width):
    h = _rms(mem_ref[...], g_ref[0]).astype(BF16)
    kv = _dot(h, w_ref[0])
    k_ref[0] = _head_rms(kv[:, :width], gk_ref[0], ind_ref[...])
    v_ref[0] = kv[:, width:]


def _memory_kv(mem, g_norm, w_kv, g_k, ind):
    depth, d, w2 = w_kv.shape
    width = w2 // 2
    n = mem.shape[0]
    return pl.pallas_call(
        functools.partial(_memkv_kernel, width=width),
        grid=(depth,),
        in_specs=[_full((n, d)),
                  pl.BlockSpec((1, 1, d), lambda l: (l, 0, 0)),
                  pl.BlockSpec((1, d, w2), lambda l: (l, 0, 0)),
                  pl.BlockSpec((1, 1, width), lambda l: (l, 0, 0)),
                  _full((width, width))],
        out_specs=[pl.BlockSpec((1, n, width), lambda l: (l, 0, 0))] * 2,
        out_shape=[jax.ShapeDtypeStruct((depth, n, width), F32)] * 2,
        compiler_params=_params("arbitrary"),
        name="memory_kv",
    )(mem, g_norm, w_kv, g_k, ind)


def _proj_a_kernel(x_ref, g_ref, w_ref, gq_ref, ind_ref, glu_ref, qm_ref, *, c):
    h = _rms(x_ref[0], g_ref[...]).astype(BF16)
    p = _dot(h, w_ref[...])
    glu_ref[0] = p[:, :c] * jax.nn.sigmoid(p[:, c:2 * c])
    qm_ref[0] = _head_rms(p[:, 2 * c:], gq_ref[...], ind_ref[...])


def _proj_a(x, g, w, gq, ind, c, tt):
    b, t, d = x.shape
    wm = w.shape[1] - 2 * c
    tile = lambda width: pl.BlockSpec((1, tt, width), lambda i, j: (i, j, 0))
    return pl.pallas_call(
        functools.partial(_proj_a_kernel, c=c),
        grid=(b, t // tt),
        in_specs=[tile(d), _full((1, d)), _full(w.shape), _full((1, wm)), _full((wm, wm))],
        out_specs=[tile(c), tile(wm)],
        out_shape=[jax.ShapeDtypeStruct((b, t, c), F32), jax.ShapeDtypeStruct((b, t, wm), F32)],
        compiler_params=_params("arbitrary", "arbitrary"),
        name="in_proj_a",
    )(x, g, w, gq, ind)


def _proj_b_kernel(x_ref, gkv_ref, wkv_ref, gk_ref, g1_ref, wb_ref, gq_ref, gm_ref, ind_sb_ref, ind_m_ref,
                   k_ref, v_ref, q_ref, qm_ref, kb_ref, vb_ref, *, sb):
    x = x_ref[0]
    kv = _dot(_rms(x, gkv_ref[...]).astype(BF16), wkv_ref[...])
    k = _head_rms(kv[:, :sb], gk_ref[...], ind_sb_ref[...])
    k_ref[0] = k
    v_ref[0] = kv[:, sb:]
    kb_ref[0] = k.astype(BF16)
    vb_ref[0] = kv[:, sb:].astype(BF16)
    p = _dot(_rms(x, g1_ref[...]).astype(BF16), wb_ref[...])
    q_ref[0] = _head_rms(p[:, :sb], gq_ref[...], ind_sb_ref[...])
    qm_ref[0] = _head_rms(p[:, sb:], gm_ref[...], ind_m_ref[...])


def _proj_b(x, gkv, wkv, gk, g1, wb, gq, gm, ind_sb, ind_m, sb, tt):
    b, t, d = x.shape
    wm = wb.shape[1] - sb
    tile = lambda width: pl.BlockSpec((1, tt, width), lambda i, j: (i, j, 0))
    sds = lambda width, dtype=F32: jax.ShapeDtypeStruct((b, t, width), dtype)
    return pl.pallas_call(
        functools.partial(_proj_b_kernel, sb=sb),
        grid=(b, t // tt),
        in_specs=[tile(d), _full((1, d)), _full(wkv.shape), _full((1, sb)), _full((1, d)), _full(wb.shape),
                  _full((1, sb)), _full((1, wm)), _full((sb, sb)), _full((wm, wm))],
        out_specs=[tile(sb), tile(sb), tile(sb), tile(wm), tile(sb), tile(sb)],
        out_shape=[sds(sb), sds(sb), sds(sb), sds(wm), sds(sb, BF16), sds(sb, BF16)],
        compiler_params=_params("arbitrary", "arbitrary"),
        name="in_proj_b",
    )(x, gkv, wkv, gk, g1, wb, gq, gm, ind_sb, ind_m)


CONV_HALO = 32
CONV_ROWS = 64


def _conv_kernel(glu_ref, prev_ref, w_ref, b_ref, g_ref, bl_ref, out_ref, pad_ref, y_ref, *, tt, width, carry):
    c = glu_ref.shape[-1]

    @pl.when(pl.program_id(1) == 0)
    def _():
        pad_ref[0:CONV_HALO, :] = prev_ref[0]

    pad_ref[CONV_HALO:CONV_HALO + tt, :] = glu_ref[0]
    first = CONV_HALO - (width - 1)
    rows = min(CONV_ROWS, tt)
    for r0 in range(0, tt, rows):
        for l0 in range(0, c, LANES):
            acc = jnp.zeros((rows, LANES), F32) + b_ref[:, l0:l0 + LANES]
            for k in range(width):
                acc = acc + w_ref[k:k + 1, l0:l0 + LANES] * pad_ref[first + k + r0:first + k + r0 + rows,
                                                                    l0:l0 + LANES]
            y_ref[r0:r0 + rows, l0:l0 + LANES] = acc
    y = y_ref[...]
    mu = jnp.mean(y, axis=-1, keepdims=True)
    var = jnp.mean(jnp.square(y - mu), axis=-1, keepdims=True)
    yn = (y - mu) * lax.rsqrt(var + EPS) * g_ref[...] + bl_ref[...]
    out_ref[0] = yn * jax.nn.sigmoid(yn)
    if carry:
        pad_ref[0:CONV_HALO, :] = pad_ref[tt:tt + CONV_HALO, :]


def _conv(glu, prev, w, bias, g_ln, b_ln, tt):
    b, t, c = glu.shape
    width = w.shape[0]
    tile = pl.BlockSpec((1, tt, c), lambda i, j: (i, j, 0))
    return pl.pallas_call(
        functools.partial(_conv_kernel, tt=tt, width=width, carry=t > tt),
        grid=(b, t // tt),
        in_specs=[tile, pl.BlockSpec((1, CONV_HALO, c), lambda i, j: (i, 0, 0)), _full(w.shape),
                  _full((1, c)), _full((1, c)), _full((1, c))],
        out_specs=tile,
        out_shape=jax.ShapeDtypeStruct((b, t, c), F32),
        scratch_shapes=[pltpu.VMEM((CONV_HALO + tt, c), F32), pltpu.VMEM((tt, c), F32)],
        compiler_params=_params("arbitrary", "arbitrary"),
        name="conformer_conv",
    )(glu, prev, w, bias, g_ln, b_ln)


def _mix_kernel(x_ref, c_ref, qm_ref, mk_ref, mv_ref, wc_ref, wm_ref, out_ref, *, heads):
    qm = qm_ref[0]
    mk = mk_ref[0].astype(BF16)
    mv = mv_ref[0].astype(BF16)
    head_of_lane = lax.broadcasted_iota(jnp.int32, (1, qm.shape[-1]), 1) // HEAD_DIM
    m_out = jnp.zeros(qm.shape, F32)
    for h in range(heads):
        mine = head_of_lane == h
        s = _dot_nt(jnp.where(mine, qm, 0.0).astype(BF16), mk) * (HEAD_DIM ** -0.5)
        e = jnp.exp(s - jnp.max(s, axis=-1, keepdims=True))
        p = e / jnp.sum(e, axis=-1, keepdims=True)
        m_out = jnp.where(mine, _dot(p.astype(BF16), mv), m_out)
    out_ref[0] = (x_ref[0] + _dot(c_ref[0].astype(BF16), wc_ref[...])
                  + _dot(m_out.astype(BF16), wm_ref[...]))


def _mix(x, c_out, qm, mem_k, mem_v, wc, wm, tt):
    b, t, d = x.shape
    cw, mw = c_out.shape[-1], qm.shape[-1]
    n_mem = mem_k.shape[1]
    tile = lambda width: pl.BlockSpec((1, tt, width), lambda i, j: (i, j, 0))
    mem = pl.BlockSpec((1, n_mem, mw), lambda i, j: (i, 0, 0))
    return pl.pallas_call(
        functools.partial(_mix_kernel, heads=mw // HEAD_DIM),
        grid=(b, t // tt),
        in_specs=[tile(d), tile(cw), tile(mw), mem, mem, _full(wc.shape), _full(wm.shape)],
        out_specs=tile(d),
        out_shape=jax.ShapeDtypeStruct((b, t, d), F32),
        compiler_params=_params("arbitrary", "arbitrary"),
        name="mem_attn_out_proj",
    )(x, c_out, qm, mem_k, mem_v, wc, wm)


def _top16(s, key_idx, stable):
    n = s.shape[0]
    work = s
    rank = jnp.full(s.shape, float(TOPK), F32)
    vals = []
    for r in range(TOPK):
        m = jnp.max(work, axis=0, keepdims=True)
        sel = work == m
        if stable:
            sel = key_idx == jnp.min(jnp.where(sel, key_idx, float(n)), axis=0, keepdims=True)
        rank = jnp.where(sel, float(r), rank)
        work = jnp.where(sel, NEG_INF, work)
        vals.append(m)
    return vals, rank


def _pair_top16(sv1, sv2, cols, stable):
    sub = 8
    row = lax.broadcasted_iota(jnp.int32, (TOPK, cols), 0).astype(F32)
    sv2_col = jnp.zeros((TOPK, cols), F32)
    for c in range(TOPK):
        sv2_col = jnp.where(row == float(c), sv2[c], sv2_col)
    cands, flat = [], []
    for r in range(TOPK):
        nrow = TOPK if r == 0 else sub
        cands.append(jnp.where(row[:nrow] < float(TOPK // (r + 1)), sv1[r] + sv2_col[:nrow], NEG_INF))
        flat.append(row[:nrow] + float(TOPK * r))
    cand = jnp.concatenate(cands, axis=0)
    flat = jnp.concatenate(flat, axis=0)
    top = sv1[0] + sv2[0]
    kept = jnp.zeros(cand.shape, F32)
    z = jnp.zeros((1, cols), F32)
    for _ in range(TOPK):
        m = jnp.max(cand, axis=0, keepdims=True)
        sel = cand == m
        if stable:
            sel = flat == jnp.min(jnp.where(sel, flat, float(TOPK * TOPK)), axis=0, keepdims=True)
        kept = jnp.where(sel, 1.0, kept)
        cand = jnp.where(sel, NEG_INF, cand)
        z = z + jnp.exp(m - top)
    counts, o = [], 0
    for r in range(TOPK):
        nrow = TOPK if r == 0 else sub
        counts.append(jnp.sum(kept[o:o + nrow], axis=0, keepdims=True))
        o += nrow
    return counts, z


def _route_kernel(x_ref, g_ref, wqt_ref, keys_ref, hnt_ref, rho_ref, e2_ref, n1_ref, r1_ref, qt_ref, *, tt, heads):
    hnt = _rms(x_ref[...], g_ref[...]).T.astype(BF16)
    hnt_ref[...] = hnt
    qt_ref[...] = _dot(wqt_ref[...], hnt)
    key_idx = lax.broadcasted_iota(jnp.int32, (NKEYS, LANES), 0).astype(F32)

    def head(h, _):
        base = pl.multiple_of(h * 2 * NKEYS, 2 * NKEYS)
        for l0 in range(0, tt, LANES):
            ls = slice(l0, l0 + LANES)
            s1 = _dot(keys_ref[h, 0], qt_ref[pl.ds(base, NKEYS), ls].astype(BF16))
            s2 = _dot(keys_ref[h, 1], qt_ref[pl.ds(base + NKEYS, NKEYS), ls].astype(BF16))

            def select(stable):
                sv1, rank1 = _top16(s1, key_idx, stable)
                sv2, rank2 = _top16(s2, key_idx, stable)
                counts, z = _pair_top16(sv1, sv2, LANES, stable)
                n1 = jnp.zeros(s1.shape, F32)
                for r in range(TOPK):
                    n1 = jnp.where(rank1 == float(r), counts[r], n1)
                rho_ref[h, :, ls] = rank2.astype(BF16)
                n1_ref[h, :, ls] = n1
                e2_ref[h, :, ls] = (jnp.exp(s2 - sv2[0]) / z).astype(BF16)
                r1_ref[h, :, ls] = jnp.exp(s1 - sv1[0])
                ranked = lambda rank: jnp.sum(jnp.where(rank < float(TOPK), 1.0, 0.0), axis=0, keepdims=True)
                kept = counts[0]
                for r in range(1, TOPK):
                    kept = kept + counts[r]
                return (ranked(rank1) != float(TOPK)) | (ranked(rank2) != float(TOPK)) | (kept != float(TOPK))

            tied = select(stable=False)

            @pl.when(jnp.max(jnp.where(tied, 1.0, 0.0)) > 0.0)
            def _():
                select(stable=True)
        return 0

    lax.fori_loop(0, heads, head, 0)


def _route(x, g, wqt, keys, tt):
    n, d = x.shape
    heads = keys.shape[0]
    tab = pl.BlockSpec((heads, NKEYS, tt), lambda i: (0, 0, i))
    tab_shape = lambda dtype: jax.ShapeDtypeStruct((heads, NKEYS, n), dtype)
    return pl.pallas_call(
        functools.partial(_route_kernel, tt=tt, heads=heads),
        grid=(n // tt,),
        in_specs=[pl.BlockSpec((tt, d), lambda i: (i, 0)), _full((1, d)), _full(wqt.shape), _full(keys.shape)],
        out_specs=[pl.BlockSpec((d, tt), lambda i: (0, i)), tab, tab, tab, tab],
        out_shape=[jax.ShapeDtypeStruct((d, n), BF16), tab_shape(BF16), tab_shape(BF16), tab_shape(F32),
                   tab_shape(F32)],
        scratch_shapes=[pltpu.VMEM((wqt.shape[0], tt), F32)],
        compiler_params=_params("arbitrary"),
        name="peer_route",
    )(x, g, wqt, keys)


def _dense_kernel(hnt_ref, u_ref, v_ref, rho_ref, e2_ref, n1_ref, r1_ref, x_ref, out_ref, acc_ref, pre_ref,
                  p_ref, *, eb, heads):
    g = pl.program_id(1)
    n_blocks = 2 * (pl.num_programs(1) - 1)
    tt = acc_ref.shape[1]
    per = eb // NKEYS

    @pl.when(g == 0)
    def _():
        acc_ref[...] = jnp.zeros(acc_ref.shape, F32)
        pre_ref[...] = jnp.zeros(pre_ref.shape, F32)
        p_ref[...] = jnp.zeros(p_ref.shape, BF16)

    for slot in range(2):
        s = 2 * g + slot
        rows = slice(slot * eb, (slot + 1) * eb)
        acc_ref[...] += lax.dot_general(v_ref[rows, :], p_ref[slot], (((0,), (0,)), ((), ())),
                                        preferred_element_type=F32)
        blk = jnp.clip(s - 1, 0, n_blocks - 1)
        for ii in range(per):
            i = blk * per + ii
            w = jnp.zeros((NKEYS, tt), BF16)
            for h in range(heads):
                kept = rho_ref[h] < n1_ref[h, pl.ds(i, 1), :].astype(BF16)
                w = w + jnp.where(kept, e2_ref[h], 0) * r1_ref[h, pl.ds(i, 1), :].astype(BF16)
            pre = pre_ref[1 - slot, ii * NKEYS:(ii + 1) * NKEYS, :]
            act = 0.5 * pre * (1.0 + lax.erf(pre * (2.0 ** -0.5)))
            p_ref[1 - slot, ii * NKEYS:(ii + 1) * NKEYS, :] = w * act.astype(BF16)
        pre_ref[slot] = _dot(u_ref[rows, :], hnt_ref[...])

    @pl.when(g == pl.num_programs(1) - 1)
    def _():
        out_ref[...] = x_ref[...] + acc_ref[...].T


def _dense(x, hnt, u, v, rho, e2, n1, r1, tt, eb):
    n, d = x.shape
    n_exp = u.shape[0]
    heads = rho.shape[0]
    pairs = n_exp // (2 * eb)
    tab = pl.BlockSpec((heads, NKEYS, tt), lambda i, s: (0, 0, i))
    return pl.pallas_call(
        functools.partial(_dense_kernel, eb=eb, heads=heads),
        grid=(n // tt, pairs + 1),
        in_specs=[pl.BlockSpec((d, tt), lambda i, s: (0, i)),
                  pl.BlockSpec((2 * eb, d), lambda i, s: (jnp.minimum(s, pairs - 1), 0)),
                  pl.BlockSpec((2 * eb, d), lambda i, s: (jnp.clip(s - 1, 0, pairs - 1), 0)),
                  tab, tab, tab, tab,
                  pl.BlockSpec((tt, d), lambda i, s: (i, 0))],
        out_specs=pl.BlockSpec((tt, d), lambda i, s: (i, 0)),
        out_shape=jax.ShapeDtypeStruct((n, d), F32),
        scratch_shapes=[pltpu.VMEM((d, tt), F32), pltpu.VMEM((2, eb, tt), F32), pltpu.VMEM((2, eb, tt), BF16)],
        compiler_params=_params("arbitrary", "arbitrary"),
        name="peer_dense",
    )(hnt, u, v, rho, e2, n1, r1, x)


def _peer(x, g, wqt, keys, u, v, tt_route, tt_dense, eb):
    hnt, rho, e2, n1, r1 = _route(x, g, wqt, keys, tt_route)
    return _dense(x, hnt, u, v, rho, e2, n1, r1, tt_dense, eb)


def _sb_prompt_kernel(bias_ref, q_ref, k_ref, v_ref, tri_ref, o_ref, *, tq):
    hg = pl.program_id(1)
    qi = pl.program_id(2)
    groups = q_ref.shape[-1] // LANES
    first_head = lax.broadcasted_iota(jnp.int32, (1, LANES), 1) < HEAD_DIM
    strict = (lax.broadcasted_iota(jnp.int32, (tq, tq), 1) < lax.broadcasted_iota(jnp.int32, (tq, tq), 0))
    strict = jnp.concatenate([strict, strict], axis=0)
    tri = tri_ref[...]
    q2, bias = [], []
    for g in range(groups):
        q = q_ref[0, :, g * LANES:(g + 1) * LANES] * (HEAD_DIM ** -0.5)
        q2.append(jnp.concatenate([jnp.where(first_head, q, 0.0), jnp.where(first_head, 0.0, q)],
                                  axis=0).astype(BF16))
        bias.append((bias_ref[(hg * groups + g) * 2], bias_ref[(hg * groups + g) * 2 + 1]))

    def chunk(c, state, diag):
        start = pl.multiple_of(c * tq, tq)
        out = []
        for g in range(groups):
            carry, o_acc = state[g]
            kb = k_ref[0, pl.ds(start, tq), g * LANES:(g + 1) * LANES]
            vb = v_ref[0, pl.ds(start, tq), g * LANES:(g + 1) * LANES]
            z = _dot_nt(q2[g], kb)
            z = jnp.concatenate([z[:tq] + bias[g][0], z[tq:] + bias[g][1]], axis=0)
            lk = _neg_softplus(z)
            if diag:
                lk = jnp.where(strict, lk, 0.0)
            hi, lo = _split(lk)
            parts = []
            for s in reversed(range(tq // LANES)):
                sl = slice(s * LANES, (s + 1) * LANES)
                cum = _dot(hi[:, sl], tri) + _dot(lo[:, sl], tri)
                a = jnp.exp(z[:, sl] + cum + carry)
                if diag:
                    a = jnp.where(strict[:, sl], a, 0.0)
                parts.insert(0, a.astype(BF16))
                carry = carry + jnp.sum(lk[:, sl], axis=1, keepdims=True)
            out.append((carry, o_acc + _dot(jnp.concatenate(parts, axis=1), vb)))
        return tuple(out)

    zero = (jnp.zeros((2 * tq, 1), F32), jnp.zeros((2 * tq, LANES), F32))
    state = chunk(qi, (zero,) * groups, True)
    state = lax.fori_loop(0, qi, lambda jj, st: chunk(qi - 1 - jj, st, False), state)
    for g in range(groups):
        o_acc = state[g][1]
        o_ref[0, :, g * LANES:(g + 1) * LANES] = jnp.where(first_head, o_acc[:tq], o_acc[tq:])


SB_GROUP = 3 * LANES


def _sb_prompt(q, k, v, bias, tri, tq):
    b, t, w = q.shape
    kv = pl.BlockSpec((1, t, SB_GROUP), lambda i, hp, qi: (i, 0, hp))
    qo = pl.BlockSpec((1, tq, SB_GROUP), lambda i, hp, qi: (i, qi, hp))
    return pl.pallas_call(
        functools.partial(_sb_prompt_kernel, tq=tq),
        grid=(b, w // SB_GROUP, t // tq),
        in_specs=[pl.BlockSpec(memory_space=pltpu.SMEM), qo, kv, kv, _full(tri.shape)],
        out_specs=qo,
        out_shape=jax.ShapeDtypeStruct((b, t, w), F32),
        compiler_params=_params("arbitrary", "arbitrary", "arbitrary"),
        name="stick_breaking_prompt",
    )(bias, q, k, v, tri)


SB_ROWS = 8


SB_PAGES = 4


def _sb_sample_kernel(pt_ref, qmt_ref, bias_ref, kn_ref, vn_ref, *rest, n_new, heads):
    del pt_ref
    kp_refs, vp_refs = rest[:SB_PAGES], rest[SB_PAGES:2 * SB_PAGES]
    trit_ref, o_ref, carry_ref, acc_ref = rest[2 * SB_PAGES:]
    p = pl.program_id(1)
    page = kn_ref.shape[1]

    def head_rows(refs, h):
        return jnp.concatenate([r[0, :, h, :] for r in refs], axis=0).astype(BF16)

    def process(k_refs, v_refs, valid):
        z = bias_ref[...]
        for h in range(heads):
            z = z + _dot(head_rows(k_refs, h), qmt_ref[0, h])
        lk = _neg_softplus(z)
        if valid is not None:
            lk = jnp.where(valid, lk, 0.0)
        hi, lo = _split(lk)
        carry = carry_ref[...]
        parts = []
        for s in reversed(range(len(k_refs))):
            sl = slice(s * page, (s + 1) * page)
            cum = _dot(trit_ref[...], hi[sl]) + _dot(trit_ref[...], lo[sl])
            parts.insert(0, jnp.exp(z[sl] + cum + carry))
            carry = carry + jnp.sum(lk[sl], axis=0, keepdims=True)
        a = jnp.concatenate(parts, axis=0)
        if valid is not None:
            a = jnp.where(valid, a, 0.0)
        at = a.T
        for h in range(heads):
            rows = at[h * SB_ROWS:h * SB_ROWS + 2 * SB_ROWS].astype(BF16)
            acc_ref[h] += _dot(rows, head_rows(v_refs, h))[:SB_ROWS]
        carry_ref[...] = carry

    @pl.when(p == 0)
    def _():
        carry_ref[...] = jnp.zeros(carry_ref.shape, F32)
        acc_ref[...] = jnp.zeros(acc_ref.shape, F32)
        key = lax.broadcasted_iota(jnp.int32, (page, LANES), 0)
        query = lax.broadcasted_iota(jnp.int32, (page, LANES), 1) % SB_ROWS
        process([kn_ref], [vn_ref], (key < query) & (key < n_new))

    @pl.when(p > 0)
    def _():
        process(kp_refs, vp_refs, None)

    @pl.when(p == pl.num_programs(1) - 1)
    def _():
        o_ref[0] = acc_ref[...]


def _sb_sample(page_table, qmt, bias_cols, k_new, v_new, cache_k, cache_v, trit, n_new):
    n_seq, n_pages = page_table.shape
    _, page, heads, hd = cache_k.shape

    def page_spec(s):
        def index_map(i, p, pt):
            return (pt[i * n_pages + n_pages - jnp.maximum(p, 1) * SB_PAGES + s], 0, 0, 0)
        return pl.BlockSpec((1, page, heads, hd), index_map)

    per_seq = lambda shape: pl.BlockSpec((1,) + shape, lambda i, p, pt: (i,) + (0,) * len(shape))
    const = lambda shape: pl.BlockSpec(shape, lambda i, p, pt: (0, 0))
    pages = [page_spec(s) for s in range(SB_PAGES)]
    grid_spec = pltpu.PrefetchScalarGridSpec(
        num_scalar_prefetch=1,
        grid=(n_seq, n_pages // SB_PAGES + 1),
        in_specs=[per_seq((heads, hd, LANES)), const((1, LANES)), per_seq((page, heads, hd)),
                  per_seq((page, heads, hd))] + pages + pages + [const((page, page))],
        out_specs=per_seq((heads, SB_ROWS, hd)),
        scratch_shapes=[pltpu.VMEM((1, LANES), F32), pltpu.VMEM((heads, SB_ROWS, hd), F32)],
    )
    return pl.pallas_call(
        functools.partial(_sb_sample_kernel, n_new=n_new, heads=heads),
        grid_spec=grid_spec,
        out_shape=jax.ShapeDtypeStruct((n_seq, heads, SB_ROWS, hd), F32),
        compiler_params=_params("arbitrary", "arbitrary"),
        name="stick_breaking_sample",
    )(page_table.reshape(-1), qmt, bias_cols, k_new, v_new, *([cache_k] * SB_PAGES), *([cache_v] * SB_PAGES), trit)


def _head_indicator(width):
    head = jnp.arange(width) // HEAD_DIM
    return (head[:, None] == head[None, :]).astype(BF16)


def _tiles(t):
    seq = min(t, 256)
    return seq


def kernel(x_prompt, x_sample, mem_prompt, state_conv, cache_k, cache_v, cache_mem_k, cache_mem_v, page_table,
           g_norm1, g_norm2, g_mem_norm, w_mem_kv, g_mem_q, g_mem_k, w_in_a, w_dw, b_dw, g_conv_ln, b_conv_ln,
           w_out_a, g_kv, w_kv, g_sb_k, w_in_b, g_sb_q, sb_bias, w_out_b, peer_w_q, peer_sub_keys, peer_u,
           peer_v):
    d = x_prompt.shape[-1]
    c_conv = w_dw.shape[-1]
    conv_w = w_dw.shape[1]
    sb_w = w_kv.shape[-1] // 2
    mem_w = w_mem_kv.shape[-1] // 2
    sb_heads = sb_w // HEAD_DIM
    mem_heads = mem_w // HEAD_DIM
    n_mem = mem_prompt.shape[1]
    dec_b, dec_t, _ = x_sample.shape

    row = lambda v: v.reshape(1, -1)
    tile_g = lambda g, heads: jnp.tile(g, heads).reshape(1, -1)
    ind_m = _head_indicator(mem_w)
    ind_sb = _head_indicator(sb_w)
    w_in_a_b = w_in_a[0].astype(BF16)
    w_out_a_b = w_out_a[0].astype(BF16)
    w_in_b_b = w_in_b[0].astype(BF16)
    w_out_b_b = w_out_b[0].astype(BF16)
    w_kv_b = w_kv.astype(BF16)
    wqt = [peer_w_q[l].T.astype(BF16) for l in range(2)]
    keys = [peer_sub_keys[l].astype(BF16) for l in range(2)]
    u_b = [peer_u[l].astype(BF16) for l in range(2)]
    vt_b = [peer_v[l].astype(BF16) for l in range(2)]
    tri = (jnp.arange(LANES)[:, None] >= jnp.arange(LANES)[None, :]).astype(BF16)

    b_p, t_p, _ = x_prompt.shape
    mk, mv = _memory_kv(mem_prompt.reshape(b_p * n_mem, d), g_mem_norm.reshape(2, 1, d), w_mem_kv.astype(BF16),
                        jnp.tile(g_mem_k, (1, mem_heads)).reshape(2, 1, mem_w), ind_m)
    mem_k_prompt = mk.reshape(2, b_p, n_mem, mem_heads, HEAD_DIM)
    mem_v_prompt = mv.reshape(2, b_p, n_mem, mem_heads, HEAD_DIM)

    def forward(x, prev, mem_k, mem_v, paged):
        b, t, _ = x.shape
        tt = min(t, 256)
        n = b * t
        tt_route = min(n, 256)
        tt_dense = min(n, 512)
        glu, qm = _proj_a(x, row(g_norm1[0]), w_in_a_b, tile_g(g_mem_q[0], mem_heads), ind_m, c_conv, tt)
        c_out = _conv(glu, prev, w_dw[0], row(b_dw[0]), row(g_conv_ln[0]), row(b_conv_ln[0]), tt)
        x = _mix(x, c_out, qm, mem_k[0], mem_v[0], w_out_a_b[:c_conv], w_out_a_b[c_conv:], tt)
        x = _peer(x.reshape(n, d), row(g_norm2[0]), wqt[0], keys[0], u_b[0], vt_b[0],
                  tt_route, tt_dense, 512).reshape(b, t, d)
        k, v, q, qm, k_mxu, v_mxu = _proj_b(x, row(g_kv), w_kv_b, tile_g(g_sb_k, sb_heads), row(g_norm1[1]),
                                            w_in_b_b, tile_g(g_sb_q[0], sb_heads), tile_g(g_mem_q[1], mem_heads),
                                            ind_sb, ind_m, sb_w, tt)
        if paged is None:
            o = _sb_prompt(q, k_mxu, v_mxu, sb_bias[0], tri, min(t, 2 * LANES))
        else:
            o = paged(q, k, v)
        x = _mix(x, o, qm, mem_k[1], mem_v[1], w_out_b_b[:sb_w], w_out_b_b[sb_w:], tt)
        x = _peer(x.reshape(n, d), row(g_norm2[1]), wqt[1], keys[1], u_b[1], vt_b[1],
                  tt_route, tt_dense, 512).reshape(b, t, d)
        return x, glu, k, v

    prev0 = jnp.zeros((b_p, CONV_HALO, c_conv), F32)
    y_prompt, glu_p, k_p, v_p = forward(x_prompt, prev0, mk.reshape(2, b_p, n_mem, mem_w),
                                        mv.reshape(2, b_p, n_mem, mem_w), None)
    conv_prompt = glu_p[None, :, t_p - (conv_w - 1):, :]

    n_pool, page, _, _ = cache_k.shape
    pad_t = SB_ROWS - dec_t
    x_s = jnp.pad(x_sample, ((0, 0), (0, pad_t), (0, 0)))
    prev_s = jnp.pad(state_conv[0], ((0, 0), (CONV_HALO - (conv_w - 1), 0), (0, 0)))
    col_head = jnp.arange(LANES) // SB_ROWS
    col_query = jnp.arange(LANES) % SB_ROWS
    lane_head = jnp.arange(sb_w) // HEAD_DIM
    col_ok = (col_head < sb_heads) & (col_query < dec_t)
    place = ((lane_head[:, None] == col_head[None, :]) & col_ok[None, :]).astype(F32)
    bias_cols = jnp.where(col_head < sb_heads, sb_bias[0][jnp.minimum(col_head, sb_heads - 1)], 0.0).reshape(1, LANES)
    trit = tri.T

    def paged(q, k, v):
        q_cols = jnp.take(q * (HEAD_DIM ** -0.5), jnp.minimum(col_query, SB_ROWS - 1), axis=1)
        qmt = (jnp.swapaxes(q_cols, 1, 2) * place[None]).astype(BF16)
        qmt = qmt.reshape(dec_b, sb_heads, HEAD_DIM, LANES)
        grow = lambda a: jnp.pad(a[:, :dec_t], ((0, 0), (0, page - dec_t), (0, 0))).reshape(
            dec_b, page, sb_heads, HEAD_DIM)
        o = _sb_sample(page_table, qmt, bias_cols, grow(k), grow(v), cache_k, cache_v, trit, dec_t)
        return jnp.swapaxes(o, 1, 2).reshape(dec_b, SB_ROWS, sb_w)

    y_s, glu_s, k_s, v_s = forward(x_s, prev_s, cache_mem_k.reshape(2, dec_b, n_mem, mem_w),
                                   cache_mem_v.reshape(2, dec_b, n_mem, mem_w), paged)
    y_sample = y_s[:, :dec_t]
    conv_sample = jnp.concatenate([state_conv[0][:, dec_t:], glu_s[:, :dec_t]], axis=1)[None]
    shape_kv = lambda a: a.reshape(a.shape[0], a.shape[1], sb_heads, HEAD_DIM)
    return (y_prompt, y_sample, conv_prompt, conv_sample, shape_kv(k_p), shape_kv(v_p),
            shape_kv(k_s[:, :dec_t]), shape_kv(v_s[:, :dec_t]), mem_k_prompt, mem_v_prompt)
```

```python
import functools

import jax
import jax.numpy as jnp
from jax import lax
from jax.experimental import pallas as pl
from jax.experimental.pallas import tpu as pltpu

F32 = jnp.float32
BF16 = jnp.bfloat16

HEAD_DIM = 64
EPS = 1e-6
TOPK = 16
NKEYS = 128
LANES = 128
VMEM_LIMIT = 56 * 1024 * 1024
NEG_INF = float("-inf")


def _params(*sem, **kw):
    return pltpu.CompilerParams(dimension_semantics=sem, vmem_limit_bytes=VMEM_LIMIT, **kw)


def _dot(a, b):
    return jnp.dot(a, b, preferred_element_type=F32)


def _dot_nt(a, b):
    return lax.dot_general(a, b, (((1,), (1,)), ((), ())), preferred_element_type=F32)


def _split(x):
    hi = x.astype(BF16)
    return hi, (x - hi.astype(F32)).astype(BF16)


def _rms(x, g):
    return x * lax.rsqrt(jnp.mean(x * x, axis=-1, keepdims=True) + EPS) * g


def _head_rms(x, g, ind):
    hi, lo = _split(x * x)
    ms = (_dot(hi, ind) + _dot(lo, ind)) * (1.0 / HEAD_DIM)
    return x * lax.rsqrt(ms + EPS) * g


def _neg_softplus(z):
    return -(jnp.maximum(z, 0.0) + jnp.log(1.0 + jnp.exp(-jnp.abs(z))))


def _full(shape):
    n = len(shape)
    return pl.BlockSpec(shape, lambda *_: (0,) * n)


def _memkv_kernel(mem_ref, g_ref, w_ref, gk_ref, ind_ref, k_ref, v_ref, *, width):
    h = _rms(mem_ref[...], g_ref[0]).astype(BF16)
    kv = _dot(h, w_ref[0])
    k_ref[0] = _head_rms(kv[:, :width], gk_ref[0], ind_ref[...])
    v_ref[0] = kv[:, width:]


def _memory_kv(mem, g_norm, w_kv, g_k, ind):
    depth, d, w2 = w_kv.shape
    width = w2 // 2
    n = mem.shape[0]
    return pl.pallas_call(
        functools.partial(_memkv_kernel, width=width),
        grid=(depth,),
        in_specs=[_full((n, d)),
                  pl.BlockSpec((1, 1, d), lambda l: (l, 0, 0)),
                  pl.BlockSpec((1, d, w2), lambda l: (l, 0, 0)),
                  pl.BlockSpec((1, 1, width), lambda l: (l, 0, 0)),
                  _full((width, width))],
        out_specs=[pl.BlockSpec((1, n, width), lambda l: (l, 0, 0))] * 2,
        out_shape=[jax.ShapeDtypeStruct((depth, n, width), F32)] * 2,
        compiler_params=_params("arbitrary"),
        name="memory_kv",
    )(mem, g_norm, w_kv, g_k, ind)


def _proj_a_kernel(x_ref, g_ref, w_ref, gq_ref, ind_ref, glu_ref, qm_ref, *, c):
    h = _rms(x_ref[0], g_ref[...]).astype(BF16)
    p = _dot(h, w_ref[...])
    glu_ref[0] = p[:, :c] * jax.nn.sigmoid(p[:, c:2 * c])
    qm_ref[0] = _head_rms(p[:, 2 * c:], gq_ref[...], ind_ref[...])


def _proj_a(x, g, w, gq, ind, c, tt):
    b, t, d = x.shape
    wm = w.shape[1] - 2 * c
    tile = lambda width: pl.BlockSpec((1, tt, width), lambda i, j: (i, j, 0))
    return pl.pallas_call(
        functools.partial(_proj_a_kernel, c=c),
        grid=(b, t // tt),
        in_specs=[tile(d), _full((1, d)), _full(w.shape), _full((1, wm)), _full((wm, wm))],
        out_specs=[tile(c), tile(wm)],
        out_shape=[jax.ShapeDtypeStruct((b, t, c), F32), jax.ShapeDtypeStruct((b, t, wm), F32)],
        compiler_params=_params("arbitrary", "arbitrary"),
        name="in_proj_a",
    )(x, g, w, gq, ind)


def _proj_b_kernel(x_ref, gkv_ref, wkv_ref, gk_ref, g1_ref, wb_ref, gq_ref, gm_ref, ind_sb_ref, ind_m_ref,
                   k_ref, v_ref, q_ref, qm_ref, kb_ref, vb_ref, *, sb):
    x = x_ref[0]
    kv = _dot(_rms(x, gkv_ref[...]).astype(BF16), wkv_ref[...])
    k = _head_rms(kv[:, :sb], gk_ref[...], ind_sb_ref[...])
    k_ref[0] = k
    v_ref[0] = kv[:, sb:]
    kb_ref[0] = k.astype(BF16)
    vb_ref[0] = kv[:, sb:].astype(BF16)
    p = _dot(_rms(x, g1_ref[...]).astype(BF16), wb_ref[...])
    q_ref[0] = _head_rms(p[:, :sb], gq_ref[...], ind_sb_ref[...])
    qm_ref[0] = _head_rms(p[:, sb:], gm_ref[...], ind_m_ref[...])


def _proj_b(x, gkv, wkv, gk, g1, wb, gq, gm, ind_sb, ind_m, sb, tt):
    b, t, d = x.shape
    wm = wb.shape[1] - sb
    tile = lambda width: pl.BlockSpec((1, tt, width), lambda i, j: (i, j, 0))
    sds = lambda width, dtype=F32: jax.ShapeDtypeStruct((b, t, width), dtype)
    return pl.pallas_call(
        functools.partial(_proj_b_kernel, sb=sb),
        grid=(b, t // tt),
        in_specs=[tile(d), _full((1, d)), _full(wkv.shape), _full((1, sb)), _full((1, d)), _full(wb.shape),
                  _full((1, sb)), _full((1, wm)), _full((sb, sb)), _full((wm, wm))],
        out_specs=[tile(sb), tile(sb), tile(sb), tile(wm), tile(sb), tile(sb)],
        out_shape=[sds(sb), sds(sb), sds(sb), sds(wm), sds(sb, BF16), sds(sb, BF16)],
        compiler_params=_params("arbitrary", "arbitrary"),
        name="in_proj_b",
    )(x, gkv, wkv, gk, g1, wb, gq, gm, ind_sb, ind_m)


CONV_HALO = 32
CONV_ROWS = 64


def _conv_kernel(glu_ref, prev_ref, w_ref, b_ref, g_ref, bl_ref, out_ref, pad_ref, y_ref, *, tt, width, carry):
    c = glu_ref.shape[-1]

    @pl.when(pl.program_id(1) == 0)
    def _():
        pad_ref[0:CONV_HALO, :] = prev_ref[0]

    pad_ref[CONV_HALO:CONV_HALO + tt, :] = glu_ref[0]
    first = CONV_HALO - (width - 1)
    rows = min(CONV_ROWS, tt)
    for r0 in range(0, tt, rows):
        for l0 in range(0, c, LANES):
            acc = jnp.zeros((rows, LANES), F32) + b_ref[:, l0:l0 + LANES]
            for k in range(width):
                acc = acc + w_ref[k:k + 1, l0:l0 + LANES] * pad_ref[first + k + r0:first + k + r0 + rows,
                                                                    l0:l0 + LANES]
            y_ref[r0:r0 + rows, l0:l0 + LANES] = acc
    y = y_ref[...]
    mu = jnp.mean(y, axis=-1, keepdims=True)
    var = jnp.mean(jnp.square(y - mu), axis=-1, keepdims=True)
    yn = (y - mu) * lax.rsqrt(var + EPS) * g_ref[...] + bl_ref[...]
    out_ref[0] = yn * jax.nn.sigmoid(yn)
    if carry:
        pad_ref[0:CONV_HALO, :] = pad_ref[tt:tt + CONV_HALO, :]


def _conv(glu, prev, w, bias, g_ln, b_ln, tt):
    b, t, c = glu.shape
    width = w.shape[0]
    tile = pl.BlockSpec((1, tt, c), lambda i, j: (i, j, 0))
    return pl.pallas_call(
        functools.partial(_conv_kernel, tt=tt, width=width, carry=t > tt),
        grid=(b, t // tt),
        in_specs=[tile, pl.BlockSpec((1, CONV_HALO, c), lambda i, j: (i, 0, 0)), _full(w.shape),
                  _full((1, c)), _full((1, c)), _full((1, c))],
        out_specs=tile,
        out_shape=jax.ShapeDtypeStruct((b, t, c), F32),
        scratch_shapes=[pltpu.VMEM((CONV_HALO + tt, c), F32), pltpu.VMEM((tt, c), F32)],
        compiler_params=_params("arbitrary", "arbitrary"),
        name="conformer_conv",
    )(glu, prev, w, bias, g_ln, b_ln)


def _mix_kernel(x_ref, c_ref, qm_ref, mk_ref, mv_ref, wc_ref, wm_ref, out_ref, *, heads):
    qm = qm_ref[0]
    mk = mk_ref[0].astype(BF16)
    mv = mv_ref[0].astype(BF16)
    head_of_lane = lax.broadcasted_iota(jnp.int32, (1, qm.shape[-1]), 1) // HEAD_DIM
    m_out = jnp.zeros(qm.shape, F32)
    for h in range(heads):
        mine = head_of_lane == h
        s = _dot_nt(jnp.where(mine, qm, 0.0).astype(BF16), mk) * (HEAD_DIM ** -0.5)
        e = jnp.exp(s - jnp.max(s, axis=-1, keepdims=True))
        p = e / jnp.sum(e, axis=-1, keepdims=True)
        m_out = jnp.where(mine, _dot(p.astype(BF16), mv), m_out)
    out_ref[0] = (x_ref[0] + _dot(c_ref[0].astype(BF16), wc_ref[...])
                  + _dot(m_out.astype(BF16), wm_ref[...]))


def _mix(x, c_out, qm, mem_k, mem_v, wc, wm, tt):
    b, t, d = x.shape
    cw, mw = c_out.shape[-1], qm.shape[-1]
    n_mem = mem_k.shape[1]
    tile = lambda width: pl.BlockSpec((1, tt, width), lambda i, j: (i, j, 0))
    mem = pl.BlockSpec((1, n_mem, mw), lambda i, j: (i, 0, 0))
    return pl.pallas_call(
        functools.partial(_mix_kernel, heads=mw // HEAD_DIM),
        grid=(b, t // tt),
        in_specs=[tile(d), tile(cw), tile(mw), mem, mem, _full(wc.shape), _full(wm.shape)],
        out_specs=tile(d),
        out_shape=jax.ShapeDtypeStruct((b, t, d), F32),
        compiler_params=_params("arbitrary", "arbitrary"),
        name="mem_attn_out_proj",
    )(x, c_out, qm, mem_k, mem_v, wc, wm)


def _top16(s, key_idx, stable):
    n = s.shape[0]
    work = s
    rank = jnp.full(s.shape, float(TOPK), F32)
    vals = []
    for r in range(TOPK):
        m = jnp.max(work, axis=0, keepdims=True)
        sel = work == m
        if stable:
            sel = key_idx == jnp.min(jnp.where(sel, key_idx, float(n)), axis=0, keepdims=True)
        rank = jnp.where(sel, float(r), rank)
        work = jnp.where(sel, NEG_INF, work)
        vals.append(m)
    return vals, rank


def _pair_top16(sv1, sv2, cols, stable):
    sub = 8
    row = lax.broadcasted_iota(jnp.int32, (TOPK, cols), 0).astype(F32)
    sv2_col = jnp.zeros((TOPK, cols), F32)
    for c in range(TOPK):
        sv2_col = jnp.where(row == float(c), sv2[c], sv2_col)
    cands, flat = [], []
    for r in range(TOPK):
        nrow = TOPK if r == 0 else sub
        cands.append(jnp.where(row[:nrow] < float(TOPK // (r + 1)), sv1[r] + sv2_col[:nrow], NEG_INF))
        flat.append(row[:nrow] + float(TOPK * r))
    cand = jnp.concatenate(cands, axis=0)
    flat = jnp.concatenate(flat, axis=0)
    top = sv1[0] + sv2[0]
    kept = jnp.zeros(cand.shape, F32)
    z = jnp.zeros((1, cols), F32)
    for _ in range(TOPK):
        m = jnp.max(cand, axis=0, keepdims=True)
        sel = cand == m
        if stable:
            sel = flat == jnp.min(jnp.where(sel, flat, float(TOPK * TOPK)), axis=0, keepdims=True)
        kept = jnp.where(sel, 1.0, kept)
        cand = jnp.where(sel, NEG_INF, cand)
        z = z + jnp.exp(m - top)
    counts, o = [], 0
    for r in range(TOPK):
        nrow = TOPK if r == 0 else sub
        counts.append(jnp.sum(kept[o:o + nrow], axis=0, keepdims=True))
        o += nrow
    return counts, z


def _route_kernel(x_ref, g_ref, wqt_ref, keys_ref, hnt_ref, rho_ref, e2_ref, n1_ref, r1_ref, qt_ref, *, tt, heads):
    hnt = _rms(x_ref[...], g_ref[...]).T.astype(BF16)
    hnt_ref[...] = hnt
    qt_ref[...] = _dot(wqt_ref[...], hnt)
    key_idx = lax.broadcasted_iota(jnp.int32, (NKEYS, LANES), 0).astype(F32)

    def head(h, _):
        base = pl.multiple_of(h * 2 * NKEYS, 2 * NKEYS)
        for l0 in range(0, tt, LANES):
            ls = slice(l0, l0 + LANES)
            s1 = _dot(keys_ref[h, 0], qt_ref[pl.ds(base, NKEYS), ls].astype(BF16))
            s2 = _dot(keys_ref[h, 1], qt_ref[pl.ds(base + NKEYS, NKEYS), ls].astype(BF16))

            def select(stable):
                sv1, rank1 = _top16(s1, key_idx, stable)
                sv2, rank2 = _top16(s2, key_idx, stable)
                counts, z = _pair_top16(sv1, sv2, LANES, stable)
                n1 = jnp.zeros(s1.shape, F32)
                for r in range(TOPK):
                    n1 = jnp.where(rank1 == float(r), counts[r], n1)
                rho_ref[h, :, ls] = rank2.astype(BF16)
                n1_ref[h, :, ls] = n1
                e2_ref[h, :, ls] = (jnp.exp(s2 - sv2[0]) / z).astype(BF16)
                r1_ref[h, :, ls] = jnp.exp(s1 - sv1[0])
                ranked = lambda rank: jnp.sum(jnp.where(rank < float(TOPK), 1.0, 0.0), axis=0, keepdims=True)
                kept = counts[0]
                for r in range(1, TOPK):
                    kept = kept + counts[r]
                return (ranked(rank1) != float(TOPK)) | (ranked(rank2) != float(TOPK)) | (kept != float(TOPK))

            tied = select(stable=False)

            @pl.when(jnp.max(jnp.where(tied, 1.0, 0.0)) > 0.0)
            def _():
                select(stable=True)
        return 0

    lax.fori_loop(0, heads, head, 0)


def _route(x, g, wqt, keys, tt):
    n, d = x.shape
    heads = keys.shape[0]
    tab = pl.BlockSpec((heads, NKEYS, tt), lambda i: (0, 0, i))
    tab_shape = lambda dtype: jax.ShapeDtypeStruct((heads, NKEYS, n), dtype)
    return pl.pallas_call(
        functools.partial(_route_kernel, tt=tt, heads=heads),
        grid=(n // tt,),
        in_specs=[pl.BlockSpec((tt, d), lambda i: (i, 0)), _full((1, d)), _full(wqt.shape), _full(keys.shape)],
        out_specs=[pl.BlockSpec((d, tt), lambda i: (0, i)), tab, tab, tab, tab],
        out_shape=[jax.ShapeDtypeStruct((d, n), BF16), tab_shape(BF16), tab_shape(BF16), tab_shape(F32),
                   tab_shape(F32)],
        scratch_shapes=[pltpu.VMEM((wqt.shape[0], tt), F32)],
        compiler_params=_params("arbitrary"),
        name="peer_route",
    )(x, g, wqt, keys)


def _transpose_kernel(x_ref, o_ref):
    o_ref[...] = x_ref[...].T.astype(o_ref.dtype)


def _transpose_bf16(x, rows):
    n, d = x.shape
    return pl.pallas_call(
        _transpose_kernel,
        grid=(n // rows,),
        in_specs=[pl.BlockSpec((rows, d), lambda i: (i, 0))],
        out_specs=pl.BlockSpec((d, rows), lambda i: (0, i)),
        out_shape=jax.ShapeDtypeStruct((d, n), BF16),
        compiler_params=_params("arbitrary"),
        name="transpose_cast",
    )(x)


def _dense_kernel(hnt_ref, u_ref, vt_ref, rho_ref, e2_ref, n1_ref, r1_ref, x_ref, out_ref, acc_ref, pre_ref,
                  p_ref, *, eb, heads):
    g = pl.program_id(1)
    n_blocks = 2 * (pl.num_programs(1) - 1)
    tt = acc_ref.shape[1]
    per = eb // NKEYS

    @pl.when(g == 0)
    def _():
        acc_ref[...] = jnp.zeros(acc_ref.shape, F32)
        pre_ref[...] = jnp.zeros(pre_ref.shape, F32)
        p_ref[...] = jnp.zeros(p_ref.shape, BF16)

    for slot in range(2):
        s = 2 * g + slot
        rows = slice(slot * eb, (slot + 1) * eb)
        acc_ref[...] += _dot(vt_ref[:, rows], p_ref[slot])
        blk = jnp.clip(s - 1, 0, n_blocks - 1)
        for ii in range(per):
            i = blk * per + ii
            w = jnp.zeros((NKEYS, tt), BF16)
            for h in range(heads):
                kept = rho_ref[h] < n1_ref[h, pl.ds(i, 1), :].astype(BF16)
                w = w + jnp.where(kept, e2_ref[h], 0) * r1_ref[h, pl.ds(i, 1), :].astype(BF16)
            pre = pre_ref[1 - slot, ii * NKEYS:(ii + 1) * NKEYS, :]
            act = 0.5 * pre * (1.0 + lax.erf(pre * (2.0 ** -0.5)))
            p_ref[1 - slot, ii * NKEYS:(ii + 1) * NKEYS, :] = w * act.astype(BF16)
        pre_ref[slot] = _dot(u_ref[rows, :], hnt_ref[...])

    @pl.when(g == pl.num_programs(1) - 1)
    def _():
        out_ref[...] = x_ref[...] + acc_ref[...].T


def _dense(x, hnt, u, v, rho, e2, n1, r1, tt, eb):
    n, d = x.shape
    n_exp = u.shape[0]
    heads = rho.shape[0]
    pairs = n_exp // (2 * eb)
    tab = pl.BlockSpec((heads, NKEYS, tt), lambda i, s: (0, 0, i))
    return pl.pallas_call(
        functools.partial(_dense_kernel, eb=eb, heads=heads),
        grid=(n // tt, pairs + 1),
        in_specs=[pl.BlockSpec((d, tt), lambda i, s: (0, i)),
                  pl.BlockSpec((2 * eb, d), lambda i, s: (jnp.minimum(s, pairs - 1), 0)),
                  pl.BlockSpec((d, 2 * eb), lambda i, s: (0, jnp.clip(s - 1, 0, pairs - 1))),
                  tab, tab, tab, tab,
                  pl.BlockSpec((tt, d), lambda i, s: (i, 0))],
        out_specs=pl.BlockSpec((tt, d), lambda i, s: (i, 0)),
        out_shape=jax.ShapeDtypeStruct((n, d), F32),
        scratch_shapes=[pltpu.VMEM((d, tt), F32), pltpu.VMEM((2, eb, tt), F32), pltpu.VMEM((2, eb, tt), BF16)],
        compiler_params=_params("arbitrary", "arbitrary"),
        name="peer_dense",
    )(hnt, u, v, rho, e2, n1, r1, x)


def _peer(x, g, wqt, keys, u, v, tt_route, tt_dense, eb):
    hnt, rho, e2, n1, r1 = _route(x, g, wqt, keys, tt_route)
    return _dense(x, hnt, u, v, rho, e2, n1, r1, tt_dense, eb)


def _sb_prompt_kernel(bias_ref, q_ref, k_ref, v_ref, tri_ref, o_ref, *, tq):
    hg = pl.program_id(1)
    qi = pl.program_id(2)
    groups = q_ref.shape[-1] // LANES
    first_head = lax.broadcasted_iota(jnp.int32, (1, LANES), 1) < HEAD_DIM
    strict = (lax.broadcasted_iota(jnp.int32, (tq, tq), 1) < lax.broadcasted_iota(jnp.int32, (tq, tq), 0))
    strict = jnp.concatenate([strict, strict], axis=0)
    tri = tri_ref[...]
    q2, bias = [], []
    for g in range(groups):
        q = q_ref[0, :, g * LANES:(g + 1) * LANES] * (HEAD_DIM ** -0.5)
        q2.append(jnp.concatenate([jnp.where(first_head, q, 0.0), jnp.where(first_head, 0.0, q)],
                                  axis=0).astype(BF16))
        bias.append((bias_ref[(hg * groups + g) * 2], bias_ref[(hg * groups + g) * 2 + 1]))

    def chunk(c, state, diag):
        start = pl.multiple_of(c * tq, tq)
        out = []
        for g in range(groups):
            carry, o_acc = state[g]
            kb = k_ref[0, pl.ds(start, tq), g * LANES:(g + 1) * LANES]
            vb = v_ref[0, pl.ds(start, tq), g * LANES:(g + 1) * LANES]
            z = _dot_nt(q2[g], kb)
            z = jnp.concatenate([z[:tq] + bias[g][0], z[tq:] + bias[g][1]], axis=0)
            lk = _neg_softplus(z)
            if diag:
                lk = jnp.where(strict, lk, 0.0)
            hi, lo = _split(lk)
            parts = []
            for s in reversed(range(tq // LANES)):
                sl = slice(s * LANES, (s + 1) * LANES)
                cum = _dot(hi[:, sl], tri) + _dot(lo[:, sl], tri)
                a = jnp.exp(z[:, sl] + cum + carry)
                if diag:
                    a = jnp.where(strict[:, sl], a, 0.0)
                parts.insert(0, a.astype(BF16))
                carry = carry + jnp.sum(lk[:, sl], axis=1, keepdims=True)
            out.append((carry, o_acc + _dot(jnp.concatenate(parts, axis=1), vb)))
        return tuple(out)

    zero = (jnp.zeros((2 * tq, 1), F32), jnp.zeros((2 * tq, LANES), F32))
    state = chunk(qi, (zero,) * groups, True)
    state = lax.fori_loop(0, qi, lambda jj, st: chunk(qi - 1 - jj, st, False), state)
    for g in range(groups):
        o_acc = state[g][1]
        o_ref[0, :, g * LANES:(g + 1) * LANES] = jnp.where(first_head, o_acc[:tq], o_acc[tq:])


SB_GROUP = 3 * LANES


def _sb_prompt(q, k, v, bias, tri, tq):
    b, t, w = q.shape
    kv = pl.BlockSpec((1, t, SB_GROUP), lambda i, hp, qi: (i, 0, hp))
    qo = pl.BlockSpec((1, tq, SB_GROUP), lambda i, hp, qi: (i, qi, hp))
    return pl.pallas_call(
        functools.partial(_sb_prompt_kernel, tq=tq),
        grid=(b, w // SB_GROUP, t // tq),
        in_specs=[pl.BlockSpec(memory_space=pltpu.SMEM), qo, kv, kv, _full(tri.shape)],
        out_specs=qo,
        out_shape=jax.ShapeDtypeStruct((b, t, w), F32),
        compiler_params=_params("arbitrary", "arbitrary", "arbitrary"),
        name="stick_breaking_prompt",
    )(bias, q, k, v, tri)


SB_ROWS = 8


SB_PAGES = 4


def _sb_sample_kernel(pt_ref, q_ref, bias_ref, kn_ref, vn_ref, *rest, n_new, heads):
    del pt_ref
    kp_refs, vp_refs = rest[:SB_PAGES], rest[SB_PAGES:2 * SB_PAGES]
    tri_ref, bmask_ref, o_ref, carry_ref, acc_ref = rest[2 * SB_PAGES:]
    p = pl.program_id(1)
    page = kn_ref.shape[-1]
    tri = tri_ref[...]

    def process(k_refs, v_refs, valid):
        q = q_ref[0]
        zs = [_dot(q, r[0].astype(BF16)) + bias_ref[...] for r in k_refs]
        carry = carry_ref[...]
        parts = [None] * len(k_refs)
        for s in reversed(range(len(k_refs))):
            lk = _neg_softplus(zs[s])
            if valid is not None:
                lk = jnp.where(valid, lk, 0.0)
            hi, lo = _split(lk)
            a = jnp.exp(zs[s] + _dot(hi, tri) + _dot(lo, tri) + carry)
            if valid is not None:
                a = jnp.where(valid, a, 0.0)
            parts[s] = a.astype(BF16)
            carry = carry + jnp.sum(lk, axis=1, keepdims=True)
        vt = jnp.concatenate([r[0].astype(BF16) for r in v_refs], axis=1)
        acc_ref[...] += _dot_nt(jnp.concatenate(parts, axis=1), vt)
        carry_ref[...] = carry

    @pl.when(p == 0)
    def _():
        carry_ref[...] = jnp.zeros(carry_ref.shape, F32)
        acc_ref[...] = jnp.zeros(acc_ref.shape, F32)
        query = lax.broadcasted_iota(jnp.int32, (LANES, page), 0) % SB_ROWS
        key = lax.broadcasted_iota(jnp.int32, (LANES, page), 1)
        process([kn_ref], [vn_ref], (key < query) & (key < n_new))

    @pl.when(p > 0)
    def _():
        process(kp_refs, vp_refs, None)

    @pl.when(p == pl.num_programs(1) - 1)
    def _():
        own = acc_ref[...] * bmask_ref[...]
        out = own[0:SB_ROWS]
        for h in range(1, heads):
            out = out + own[h * SB_ROWS:(h + 1) * SB_ROWS]
        o_ref[0] = out


def _sb_sample(page_table, q_rows, bias_rows, kt_new, vt_new, cache_kt, cache_vt, tri, bmask, n_new, heads):
    n_seq, n_pages = page_table.shape
    _, w, page = cache_kt.shape

    def page_spec(s):
        def index_map(i, p, pt):
            return (pt[i * n_pages + n_pages - jnp.maximum(p, 1) * SB_PAGES + s], 0, 0)
        return pl.BlockSpec((1, w, page), index_map)

    per_seq = lambda shape: pl.BlockSpec((1,) + shape, lambda i, p, pt: (i, 0, 0))
    const = lambda shape: pl.BlockSpec(shape, lambda i, p, pt: (0, 0))
    pages = [page_spec(s) for s in range(SB_PAGES)]
    grid_spec = pltpu.PrefetchScalarGridSpec(
        num_scalar_prefetch=1,
        grid=(n_seq, n_pages // SB_PAGES + 1),
        in_specs=[per_seq((LANES, w)), const((LANES, page)), per_seq((w, page)), per_seq((w, page))]
                 + pages + pages + [const((page, page)), const((LANES, w))],
        out_specs=per_seq((SB_ROWS, w)),
        scratch_shapes=[pltpu.VMEM((LANES, 1), F32), pltpu.VMEM((LANES, w), F32)],
    )
    return pl.pallas_call(
        functools.partial(_sb_sample_kernel, n_new=n_new, heads=heads),
        grid_spec=grid_spec,
        out_shape=jax.ShapeDtypeStruct((n_seq, SB_ROWS, w), F32),
        compiler_params=_params("arbitrary", "arbitrary"),
        name="stick_breaking_sample",
    )(page_table.reshape(-1), q_rows, bias_rows, kt_new, vt_new, *([cache_kt] * SB_PAGES),
      *([cache_vt] * SB_PAGES), tri, bmask)


def _head_indicator(width):
    head = jnp.arange(width) // HEAD_DIM
    return (head[:, None] == head[None, :]).astype(BF16)


def _tiles(t):
    seq = min(t, 256)
    return seq


def kernel(x_prompt, x_sample, mem_prompt, state_conv, cache_k, cache_v, cache_mem_k, cache_mem_v, page_table,
           g_norm1, g_norm2, g_mem_norm, w_mem_kv, g_mem_q, g_mem_k, w_in_a, w_dw, b_dw, g_conv_ln, b_conv_ln,
           w_out_a, g_kv, w_kv, g_sb_k, w_in_b, g_sb_q, sb_bias, w_out_b, peer_w_q, peer_sub_keys, peer_u,
           peer_v):
    d = x_prompt.shape[-1]
    c_conv = w_dw.shape[-1]
    conv_w = w_dw.shape[1]
    sb_w = w_kv.shape[-1] // 2
    mem_w = w_mem_kv.shape[-1] // 2
    sb_heads = sb_w // HEAD_DIM
    mem_heads = mem_w // HEAD_DIM
    n_mem = mem_prompt.shape[1]
    dec_b, dec_t, _ = x_sample.shape

    row = lambda v: v.reshape(1, -1)
    tile_g = lambda g, heads: jnp.tile(g, heads).reshape(1, -1)
    ind_m = _head_indicator(mem_w)
    ind_sb = _head_indicator(sb_w)
    w_in_a_b = w_in_a[0].astype(BF16)
    w_out_a_b = w_out_a[0].astype(BF16)
    w_in_b_b = w_in_b[0].astype(BF16)
    w_out_b_b = w_out_b[0].astype(BF16)
    w_kv_b = w_kv.astype(BF16)
    wqt = [peer_w_q[l].T.astype(BF16) for l in range(2)]
    keys = [peer_sub_keys[l].astype(BF16) for l in range(2)]
    u_b = [peer_u[l].astype(BF16) for l in range(2)]
    vt_b = [_transpose_bf16(peer_v[l], 512) for l in range(2)]
    tri = (jnp.arange(LANES)[:, None] >= jnp.arange(LANES)[None, :]).astype(BF16)

    b_p, t_p, _ = x_prompt.shape
    mk, mv = _memory_kv(mem_prompt.reshape(b_p * n_mem, d), g_mem_norm.reshape(2, 1, d), w_mem_kv.astype(BF16),
                        jnp.tile(g_mem_k, (1, mem_heads)).reshape(2, 1, mem_w), ind_m)
    mem_k_prompt = mk.reshape(2, b_p, n_mem, mem_heads, HEAD_DIM)
    mem_v_prompt = mv.reshape(2, b_p, n_mem, mem_heads, HEAD_DIM)

    def forward(x, prev, mem_k, mem_v, paged):
        b, t, _ = x.shape
        tt = min(t, 256)
        n = b * t
        tt_route = min(n, 256)
        tt_dense = min(n, 512)
        glu, qm = _proj_a(x, row(g_norm1[0]), w_in_a_b, tile_g(g_mem_q[0], mem_heads), ind_m, c_conv, tt)
        c_out = _conv(glu, prev, w_dw[0], row(b_dw[0]), row(g_conv_ln[0]), row(b_conv_ln[0]), tt)
        x = _mix(x, c_out, qm, mem_k[0], mem_v[0], w_out_a_b[:c_conv], w_out_a_b[c_conv:], tt)
        x = _peer(x.reshape(n, d), row(g_norm2[0]), wqt[0], keys[0], u_b[0], vt_b[0],
                  tt_route, tt_dense, 512).reshape(b, t, d)
        k, v, q, qm, k_mxu, v_mxu = _proj_b(x, row(g_kv), w_kv_b, tile_g(g_sb_k, sb_heads), row(g_norm1[1]),
                                            w_in_b_b, tile_g(g_sb_q[0], sb_heads), tile_g(g_mem_q[1], mem_heads),
                                            ind_sb, ind_m, sb_w, tt)
        if paged is None:
            o = _sb_prompt(q, k_mxu, v_mxu, sb_bias[0], tri, min(t, 2 * LANES))
        else:
            o = paged(q, k, v)
        x = _mix(x, o, qm, mem_k[1], mem_v[1], w_out_b_b[:sb_w], w_out_b_b[sb_w:], tt)
        x = _peer(x.reshape(n, d), row(g_norm2[1]), wqt[1], keys[1], u_b[1], vt_b[1],
                  tt_route, tt_dense, 512).reshape(b, t, d)
        return x, glu, k, v

    prev0 = jnp.zeros((b_p, CONV_HALO, c_conv), F32)
    y_prompt, glu_p, k_p, v_p = forward(x_prompt, prev0, mk.reshape(2, b_p, n_mem, mem_w),
                                        mv.reshape(2, b_p, n_mem, mem_w), None)
    conv_prompt = glu_p[None, :, t_p - (conv_w - 1):, :]

    n_pool, page, _, _ = cache_k.shape
    pad_t = SB_ROWS - dec_t
    x_s = jnp.pad(x_sample, ((0, 0), (0, pad_t), (0, 0)))
    prev_s = jnp.pad(state_conv[0], ((0, 0), (CONV_HALO - (conv_w - 1), 0), (0, 0)))
    row_head = jnp.arange(LANES) // SB_ROWS
    row_query = jnp.arange(LANES) % SB_ROWS
    lane_head = jnp.arange(sb_w) // HEAD_DIM
    row_ok = (row_head < sb_heads) & (row_query < dec_t)
    bmask = ((row_head[:, None] == lane_head[None, :]) & row_ok[:, None]).astype(F32)
    bias_rows = jnp.where(row_head < sb_heads, sb_bias[0][jnp.minimum(row_head, sb_heads - 1)], 0.0)
    bias_rows = jnp.broadcast_to(bias_rows[:, None], (LANES, page))
    cache_kt = jnp.transpose(cache_k, (0, 2, 3, 1)).reshape(n_pool, sb_w, page)
    cache_vt = jnp.transpose(cache_v, (0, 2, 3, 1)).reshape(n_pool, sb_w, page)

    def paged(q, k, v):
        q_rows = (jnp.take(q * (HEAD_DIM ** -0.5), row_query, axis=1) * bmask[None]).astype(BF16)
        grow = lambda a: jnp.swapaxes(jnp.pad(a[:, :dec_t], ((0, 0), (0, page - dec_t), (0, 0))), 1, 2)
        return _sb_sample(page_table, q_rows, bias_rows, grow(k), grow(v), cache_kt, cache_vt, tri, bmask,
                          dec_t, sb_heads)

    y_s, glu_s, k_s, v_s = forward(x_s, prev_s, cache_mem_k.reshape(2, dec_b, n_mem, mem_w),
                                   cache_mem_v.reshape(2, dec_b, n_mem, mem_w), paged)
    y_sample = y_s[:, :dec_t]
    conv_sample = jnp.concatenate([state_conv[0][:, dec_t:], glu_s[:, :dec_t]], axis=1)[None]
    shape_kv = lambda a: a.reshape(a.shape[0], a.shape[1], sb_heads, HEAD_DIM)
    return (y_prompt, y_sample, conv_prompt, conv_sample, shape_kv(k_p), shape_kv(v_p),
            shape_kv(k_s[:, :dec_t]), shape_kv(v_s[:, :dec_t]), mem_k_prompt, mem_v_prompt)
```

```python
import functools

import jax
import jax.numpy as jnp
from jax import lax
from jax.experimental import pallas as pl
from jax.experimental.pallas import tpu as pltpu

F32 = jnp.float32
BF16 = jnp.bfloat16

HEAD_DIM = 64
EPS = 1e-6
TOPK = 16
NKEYS = 128
LANES = 128
VMEM_LIMIT = 56 * 1024 * 1024
NEG_INF = float("-inf")


def _params(*sem, **kw):
    return pltpu.CompilerParams(dimension_semantics=sem, vmem_limit_bytes=VMEM_LIMIT, **kw)


def _dot(a, b):
    return jnp.dot(a, b, preferred_element_type=F32)


def _dot_nt(a, b):
    return lax.dot_general(a, b, (((1,), (1,)), ((), ())), preferred_element_type=F32)


def _split(x):
    hi = x.astype(BF16)
    return hi, (x - hi.astype(F32)).astype(BF16)


def _rms(x, g):
    return x * lax.rsqrt(jnp.mean(x * x, axis=-1, keepdims=True) + EPS) * g


def _head_rms(x, g, ind):
    hi, lo = _split(x * x)
    ms = (_dot(hi, ind) + _dot(lo, ind)) * (1.0 / HEAD_DIM)
    return x * lax.rsqrt(ms + EPS) * g


def _neg_softplus(z):
    return -(jnp.maximum(z, 0.0) + jnp.log(1.0 + jnp.exp(-jnp.abs(z))))


def _full(shape):
    n = len(shape)
    return pl.BlockSpec(shape, lambda *_: (0,) * n)


def _memkv_kernel(mem_ref, g_ref, w_ref, gk_ref, ind_ref, k_ref, v_ref, *, width):
    h = _rms(mem_ref[...], g_ref[0]).astype(BF16)
    kv = _dot(h, w_ref[0])
    k_ref[0] = _head_rms(kv[:, :width], gk_ref[0], ind_ref[...])
    v_ref[0] = kv[:, width:]


def _memory_kv(mem, g_norm, w_kv, g_k, ind):
    depth, d, w2 = w_kv.shape
    width = w2 // 2
    n = mem.shape[0]
    return pl.pallas_call(
        functools.partial(_memkv_kernel, width=width),
        grid=(depth,),
        in_specs=[_full((n, d)),
                  pl.BlockSpec((1, 1, d), lambda l: (l, 0, 0)),
                  pl.BlockSpec((1, d, w2), lambda l: (l, 0, 0)),
                  pl.BlockSpec((1, 1, width), lambda l: (l, 0, 0)),
                  _full((width, width))],
        out_specs=[pl.BlockSpec((1, n, width), lambda l: (l, 0, 0))] * 2,
        out_shape=[jax.ShapeDtypeStruct((depth, n, width), F32)] * 2,
        compiler_params=_params("arbitrary"),
        name="memory_kv",
    )(mem, g_norm, w_kv, g_k, ind)


def _proj_a_kernel(x_ref, g_ref, w_ref, gq_ref, ind_ref, glu_ref, qm_ref, *, c):
    h = _rms(x_ref[0], g_ref[...]).astype(BF16)
    p = _dot(h, w_ref[...])
    glu_ref[0] = p[:, :c] * jax.nn.sigmoid(p[:, c:2 * c])
    qm_ref[0] = _head_rms(p[:, 2 * c:], gq_ref[...], ind_ref[...])


def _proj_a(x, g, w, gq, ind, c, tt):
    b, t, d = x.shape
    wm = w.shape[1] - 2 * c
    tile = lambda width: pl.BlockSpec((1, tt, width), lambda i, j: (i, j, 0))
    return pl.pallas_call(
        functools.partial(_proj_a_kernel, c=c),
        grid=(b, t // tt),
        in_specs=[tile(d), _full((1, d)), _full(w.shape), _full((1, wm)), _full((wm, wm))],
        out_specs=[tile(c), tile(wm)],
        out_shape=[jax.ShapeDtypeStruct((b, t, c), F32), jax.ShapeDtypeStruct((b, t, wm), F32)],
        compiler_params=_params("arbitrary", "arbitrary"),
        name="in_proj_a",
    )(x, g, w, gq, ind)


def _proj_b_kernel(x_ref, gkv_ref, wkv_ref, gk_ref, g1_ref, wb_ref, gq_ref, gm_ref, ind_sb_ref, ind_m_ref,
                   k_ref, v_ref, q_ref, qm_ref, kb_ref, vb_ref, *, sb):
    x = x_ref[0]
    kv = _dot(_rms(x, gkv_ref[...]).astype(BF16), wkv_ref[...])
    k = _head_rms(kv[:, :sb], gk_ref[...], ind_sb_ref[...])
    k_ref[0] = k
    v_ref[0] = kv[:, sb:]
    kb_ref[0] = k.astype(BF16)
    vb_ref[0] = kv[:, sb:].astype(BF16)
    p = _dot(_rms(x, g1_ref[...]).astype(BF16), wb_ref[...])
    q_ref[0] = _head_rms(p[:, :sb], gq_ref[...], ind_sb_ref[...])
    qm_ref[0] = _head_rms(p[:, sb:], gm_ref[...], ind_m_ref[...])


def _proj_b(x, gkv, wkv, gk, g1, wb, gq, gm, ind_sb, ind_m, sb, tt):
    b, t, d = x.shape
    wm = wb.shape[1] - sb
    tile = lambda width: pl.BlockSpec((1, tt, width), lambda i, j: (i, j, 0))
    sds = lambda width, dtype=F32: jax.ShapeDtypeStruct((b, t, width), dtype)
    return pl.pallas_call(
        functools.partial(_proj_b_kernel, sb=sb),
        grid=(b, t // tt),
        in_specs=[tile(d), _full((1, d)), _full(wkv.shape), _full((1, sb)), _full((1, d)), _full(wb.shape),
                  _full((1, sb)), _full((1, wm)), _full((sb, sb)), _full((wm, wm))],
        out_specs=[tile(sb), tile(sb), tile(sb), tile(wm), tile(sb), tile(sb)],
        out_shape=[sds(sb), sds(sb), sds(sb), sds(wm), sds(sb, BF16), sds(sb, BF16)],
        compiler_params=_params("arbitrary", "arbitrary"),
        name="in_proj_b",
    )(x, gkv, wkv, gk, g1, wb, gq, gm, ind_sb, ind_m)


CONV_HALO = 32
CONV_ROWS = 64


def _conv_kernel(glu_ref, prev_ref, w_ref, b_ref, g_ref, bl_ref, out_ref, pad_ref, y_ref, *, tt, width, carry):
    c = glu_ref.shape[-1]

    @pl.when(pl.program_id(1) == 0)
    def _():
        pad_ref[0:CONV_HALO, :] = prev_ref[0]

    pad_ref[CONV_HALO:CONV_HALO + tt, :] = glu_ref[0]
    first = CONV_HALO - (width - 1)
    rows = min(CONV_ROWS, tt)
    for r0 in range(0, tt, rows):
        for l0 in range(0, c, LANES):
            acc = jnp.zeros((rows, LANES), F32) + b_ref[:, l0:l0 + LANES]
            for k in range(width):
                acc = acc + w_ref[k:k + 1, l0:l0 + LANES] * pad_ref[first + k + r0:first + k + r0 + rows,
                                                                    l0:l0 + LANES]
            y_ref[r0:r0 + rows, l0:l0 + LANES] = acc
    y = y_ref[...]
    mu = jnp.mean(y, axis=-1, keepdims=True)
    var = jnp.mean(jnp.square(y - mu), axis=-1, keepdims=True)
    yn = (y - mu) * lax.rsqrt(var + EPS) * g_ref[...] + bl_ref[...]
    out_ref[0] = yn * jax.nn.sigmoid(yn)
    if carry:
        pad_ref[0:CONV_HALO, :] = pad_ref[tt:tt + CONV_HALO, :]


def _conv(glu, prev, w, bias, g_ln, b_ln, tt):
    b, t, c = glu.shape
    width = w.shape[0]
    tile = pl.BlockSpec((1, tt, c), lambda i, j: (i, j, 0))
    return pl.pallas_call(
        functools.partial(_conv_kernel, tt=tt, width=width, carry=t > tt),
        grid=(b, t // tt),
        in_specs=[tile, pl.BlockSpec((1, CONV_HALO, c), lambda i, j: (i, 0, 0)), _full(w.shape),
                  _full((1, c)), _full((1, c)), _full((1, c))],
        out_specs=tile,
        out_shape=jax.ShapeDtypeStruct((b, t, c), F32),
        scratch_shapes=[pltpu.VMEM((CONV_HALO + tt, c), F32), pltpu.VMEM((tt, c), F32)],
        compiler_params=_params("arbitrary", "arbitrary"),
        name="conformer_conv",
    )(glu, prev, w, bias, g_ln, b_ln)


def _mix_kernel(x_ref, c_ref, qm_ref, mk_ref, mv_ref, wc_ref, wm_ref, out_ref, *, heads):
    qm = qm_ref[0]
    mk = mk_ref[0].astype(BF16)
    mv = mv_ref[0].astype(BF16)
    head_of_lane = lax.broadcasted_iota(jnp.int32, (1, qm.shape[-1]), 1) // HEAD_DIM
    m_out = jnp.zeros(qm.shape, F32)
    for h in range(heads):
        mine = head_of_lane == h
        s = _dot_nt(jnp.where(mine, qm, 0.0).astype(BF16), mk) * (HEAD_DIM ** -0.5)
        e = jnp.exp(s - jnp.max(s, axis=-1, keepdims=True))
        p = e / jnp.sum(e, axis=-1, keepdims=True)
        m_out = jnp.where(mine, _dot(p.astype(BF16), mv), m_out)
    out_ref[0] = (x_ref[0] + _dot(c_ref[0].astype(BF16), wc_ref[...])
                  + _dot(m_out.astype(BF16), wm_ref[...]))


def _mix(x, c_out, qm, mem_k, mem_v, wc, wm, tt):
    b, t, d = x.shape
    cw, mw = c_out.shape[-1], qm.shape[-1]
    n_mem = mem_k.shape[1]
    tile = lambda width: pl.BlockSpec((1, tt, width), lambda i, j: (i, j, 0))
    mem = pl.BlockSpec((1, n_mem, mw), lambda i, j: (i, 0, 0))
    return pl.pallas_call(
        functools.partial(_mix_kernel, heads=mw // HEAD_DIM),
        grid=(b, t // tt),
        in_specs=[tile(d), tile(cw), tile(mw), mem, mem, _full(wc.shape), _full(wm.shape)],
        out_specs=tile(d),
        out_shape=jax.ShapeDtypeStruct((b, t, d), F32),
        compiler_params=_params("arbitrary", "arbitrary"),
        name="mem_attn_out_proj",
    )(x, c_out, qm, mem_k, mem_v, wc, wm)


def _top16(s, key_idx, stable):
    n = s.shape[0]
    work = s
    rank = jnp.full(s.shape, float(TOPK), F32)
    vals = []
    for r in range(TOPK):
        m = jnp.max(work, axis=0, keepdims=True)
        sel = work == m
        if stable:
            sel = key_idx == jnp.min(jnp.where(sel, key_idx, float(n)), axis=0, keepdims=True)
        rank = jnp.where(sel, float(r), rank)
        work = jnp.where(sel, NEG_INF, work)
        vals.append(m)
    return vals, rank


def _pair_top16(sv1, sv2, cols, stable):
    sub = 8
    row = lax.broadcasted_iota(jnp.int32, (TOPK, cols), 0).astype(F32)
    sv2_col = jnp.zeros((TOPK, cols), F32)
    for c in range(TOPK):
        sv2_col = jnp.where(row == float(c), sv2[c], sv2_col)
    cands, flat = [], []
    for r in range(TOPK):
        nrow = TOPK if r == 0 else sub
        cands.append(jnp.where(row[:nrow] < float(TOPK // (r + 1)), sv1[r] + sv2_col[:nrow], NEG_INF))
        flat.append(row[:nrow] + float(TOPK * r))
    cand = jnp.concatenate(cands, axis=0)
    flat = jnp.concatenate(flat, axis=0)
    top = sv1[0] + sv2[0]
    kept = jnp.zeros(cand.shape, F32)
    z = jnp.zeros((1, cols), F32)
    for _ in range(TOPK):
        m = jnp.max(cand, axis=0, keepdims=True)
        sel = cand == m
        if stable:
            sel = flat == jnp.min(jnp.where(sel, flat, float(TOPK * TOPK)), axis=0, keepdims=True)
        kept = jnp.where(sel, 1.0, kept)
        cand = jnp.where(sel, NEG_INF, cand)
        z = z + jnp.exp(m - top)
    counts, o = [], 0
    for r in range(TOPK):
        nrow = TOPK if r == 0 else sub
        counts.append(jnp.sum(kept[o:o + nrow], axis=0, keepdims=True))
        o += nrow
    return counts, z


def _route_kernel(x_ref, g_ref, wqt_ref, keys_ref, hnt_ref, rho_ref, e2_ref, n1_ref, r1_ref, qt_ref, *, tt, heads):
    hnt = _rms(x_ref[...], g_ref[...]).T.astype(BF16)
    hnt_ref[...] = hnt
    qt_ref[...] = _dot(wqt_ref[...], hnt)
    key_idx = lax.broadcasted_iota(jnp.int32, (NKEYS, LANES), 0).astype(F32)

    def head(h, _):
        base = pl.multiple_of(h * 2 * NKEYS, 2 * NKEYS)
        for l0 in range(0, tt, LANES):
            ls = slice(l0, l0 + LANES)
            s1 = _dot(keys_ref[h, 0], qt_ref[pl.ds(base, NKEYS), ls].astype(BF16))
            s2 = _dot(keys_ref[h, 1], qt_ref[pl.ds(base + NKEYS, NKEYS), ls].astype(BF16))

            def select(stable):
                sv1, rank1 = _top16(s1, key_idx, stable)
                sv2, rank2 = _top16(s2, key_idx, stable)
                counts, z = _pair_top16(sv1, sv2, LANES, stable)
                n1 = jnp.zeros(s1.shape, F32)
                for r in range(TOPK):
                    n1 = jnp.where(rank1 == float(r), counts[r], n1)
                rho_ref[h, :, ls] = rank2.astype(BF16)
                n1_ref[h, :, ls] = n1
                e2_ref[h, :, ls] = (jnp.exp(s2 - sv2[0]) / z).astype(BF16)
                r1_ref[h, :, ls] = jnp.exp(s1 - sv1[0])
                ranked = lambda rank: jnp.sum(jnp.where(rank < float(TOPK), 1.0, 0.0), axis=0, keepdims=True)
                kept = counts[0]
                for r in range(1, TOPK):
                    kept = kept + counts[r]
                return (ranked(rank1) != float(TOPK)) | (ranked(rank2) != float(TOPK)) | (kept != float(TOPK))

            tied = select(stable=False)

            @pl.when(jnp.max(jnp.where(tied, 1.0, 0.0)) > 0.0)
            def _():
                select(stable=True)
        return 0

    lax.fori_loop(0, heads, head, 0)


def _route(x, g, wqt, keys, tt):
    n, d = x.shape
    heads = keys.shape[0]
    tab = pl.BlockSpec((heads, NKEYS, tt), lambda i: (0, 0, i))
    tab_shape = lambda dtype: jax.ShapeDtypeStruct((heads, NKEYS, n), dtype)
    return pl.pallas_call(
        functools.partial(_route_kernel, tt=tt, heads=heads),
        grid=(n // tt,),
        in_specs=[pl.BlockSpec((tt, d), lambda i: (i, 0)), _full((1, d)), _full(wqt.shape), _full(keys.shape)],
        out_specs=[pl.BlockSpec((d, tt), lambda i: (0, i)), tab, tab, tab, tab],
        out_shape=[jax.ShapeDtypeStruct((d, n), BF16), tab_shape(BF16), tab_shape(BF16), tab_shape(F32),
                   tab_shape(F32)],
        scratch_shapes=[pltpu.VMEM((wqt.shape[0], tt), F32)],
        compiler_params=_params("arbitrary"),
        name="peer_route",
    )(x, g, wqt, keys)


def _transpose_kernel(x_ref, o_ref):
    o_ref[...] = x_ref[...].T.astype(o_ref.dtype)


def _transpose_bf16(x, rows):
    n, d = x.shape
    return pl.pallas_call(
        _transpose_kernel,
        grid=(n // rows,),
        in_specs=[pl.BlockSpec((rows, d), lambda i: (i, 0))],
        out_specs=pl.BlockSpec((d, rows), lambda i: (0, i)),
        out_shape=jax.ShapeDtypeStruct((d, n), BF16),
        compiler_params=_params("arbitrary"),
        name="transpose_cast",
    )(x)


def _dense_kernel(hnt_ref, u_ref, vt_ref, rho_ref, e2_ref, n1_ref, r1_ref, x_ref, out_ref, acc_ref, pre_ref,
                  p_ref, *, eb, heads):
    g = pl.program_id(1)
    n_blocks = 2 * (pl.num_programs(1) - 1)
    tt = acc_ref.shape[1]
    per = eb // NKEYS

    @pl.when(g == 0)
    def _():
        acc_ref[...] = jnp.zeros(acc_ref.shape, F32)
        pre_ref[...] = jnp.zeros(pre_ref.shape, F32)
        p_ref[...] = jnp.zeros(p_ref.shape, BF16)

    for slot in range(2):
        s = 2 * g + slot
        rows = slice(slot * eb, (slot + 1) * eb)
        acc_ref[...] += _dot(vt_ref[:, rows], p_ref[slot])
        blk = jnp.clip(s - 1, 0, n_blocks - 1)
        for ii in range(per):
            i = blk * per + ii
            w = jnp.zeros((NKEYS, tt), BF16)
            for h in range(heads):
                kept = rho_ref[h] < n1_ref[h, pl.ds(i, 1), :].astype(BF16)
                w = w + jnp.where(kept, e2_ref[h], 0) * r1_ref[h, pl.ds(i, 1), :].astype(BF16)
            pre = pre_ref[1 - slot, ii * NKEYS:(ii + 1) * NKEYS, :]
            act = 0.5 * pre * (1.0 + lax.erf(pre * (2.0 ** -0.5)))
            p_ref[1 - slot, ii * NKEYS:(ii + 1) * NKEYS, :] = w * act.astype(BF16)
        pre_ref[slot] = _dot(u_ref[rows, :], hnt_ref[...])

    @pl.when(g == pl.num_programs(1) - 1)
    def _():
        out_ref[...] = x_ref[...] + acc_ref[...].T


def _dense(x, hnt, u, v, rho, e2, n1, r1, tt, eb):
    n, d = x.shape
    n_exp = u.shape[0]
    heads = rho.shape[0]
    pairs = n_exp // (2 * eb)
    tab = pl.BlockSpec((heads, NKEYS, tt), lambda i, s: (0, 0, i))
    return pl.pallas_call(
        functools.partial(_dense_kernel, eb=eb, heads=heads),
        grid=(n // tt, pairs + 1),
        in_specs=[pl.BlockSpec((d, tt), lambda i, s: (0, i)),
                  pl.BlockSpec((2 * eb, d), lambda i, s: (jnp.minimum(s, pairs - 1), 0)),
                  pl.BlockSpec((d, 2 * eb), lambda i, s: (0, jnp.clip(s - 1, 0, pairs - 1))),
                  tab, tab, tab, tab,
                  pl.BlockSpec((tt, d), lambda i, s: (i, 0))],
        out_specs=pl.BlockSpec((tt, d), lambda i, s: (i, 0)),
        out_shape=jax.ShapeDtypeStruct((n, d), F32),
        scratch_shapes=[pltpu.VMEM((d, tt), F32), pltpu.VMEM((2, eb, tt), F32), pltpu.VMEM((2, eb, tt), BF16)],
        compiler_params=_params("arbitrary", "arbitrary"),
        name="peer_dense",
    )(hnt, u, v, rho, e2, n1, r1, x)


def _peer(x, g, wqt, keys, u, v, tt_route, tt_dense, eb):
    hnt, rho, e2, n1, r1 = _route(x, g, wqt, keys, tt_route)
    return _dense(x, hnt, u, v, rho, e2, n1, r1, tt_dense, eb)


def _sb_prompt_kernel(bias_ref, q_ref, k_ref, v_ref, tri_ref, o_ref, *, tq):
    hg = pl.program_id(1)
    qi = pl.program_id(2)
    groups = q_ref.shape[-1] // LANES
    first_head = lax.broadcasted_iota(jnp.int32, (1, LANES), 1) < HEAD_DIM
    strict = (lax.broadcasted_iota(jnp.int32, (tq, tq), 1) < lax.broadcasted_iota(jnp.int32, (tq, tq), 0))
    strict = jnp.concatenate([strict, strict], axis=0)
    tri = tri_ref[...]
    q2, bias = [], []
    for g in range(groups):
        q = q_ref[0, :, g * LANES:(g + 1) * LANES] * (HEAD_DIM ** -0.5)
        q2.append(jnp.concatenate([jnp.where(first_head, q, 0.0), jnp.where(first_head, 0.0, q)],
                                  axis=0).astype(BF16))
        bias.append((bias_ref[(hg * groups + g) * 2], bias_ref[(hg * groups + g) * 2 + 1]))

    def chunk(c, state, diag):
        start = pl.multiple_of(c * tq, tq)
        out = []
        for g in range(groups):
            carry, o_acc = state[g]
            kb = k_ref[0, pl.ds(start, tq), g * LANES:(g + 1) * LANES]
            vb = v_ref[0, pl.ds(start, tq), g * LANES:(g + 1) * LANES]
            z = _dot_nt(q2[g], kb)
            z = jnp.concatenate([z[:tq] + bias[g][0], z[tq:] + bias[g][1]], axis=0)
            lk = _neg_softplus(z)
            if diag:
                lk = jnp.where(strict, lk, 0.0)
            hi, lo = _split(lk)
            parts = []
            for s in reversed(range(tq // LANES)):
                sl = slice(s * LANES, (s + 1) * LANES)
                cum = _dot(hi[:, sl], tri) + _dot(lo[:, sl], tri)
                a = jnp.exp(z[:, sl] + cum + carry)
                if diag:
                    a = jnp.where(strict[:, sl], a, 0.0)
                parts.insert(0, a.astype(BF16))
                carry = carry + jnp.sum(lk[:, sl], axis=1, keepdims=True)
            out.append((carry, o_acc + _dot(jnp.concatenate(parts, axis=1), vb)))
        return tuple(out)

    zero = (jnp.zeros((2 * tq, 1), F32), jnp.zeros((2 * tq, LANES), F32))
    state = chunk(qi, (zero,) * groups, True)
    state = lax.fori_loop(0, qi, lambda jj, st: chunk(qi - 1 - jj, st, False), state)
    for g in range(groups):
        o_acc = state[g][1]
        o_ref[0, :, g * LANES:(g + 1) * LANES] = jnp.where(first_head, o_acc[:tq], o_acc[tq:])


SB_GROUP = 6 * LANES


def _sb_prompt(q, k, v, bias, tri, tq):
    b, t, w = q.shape
    kv = pl.BlockSpec((1, t, SB_GROUP), lambda i, hp, qi: (i, 0, hp), pipeline_mode=pl.Buffered(1))
    qo = pl.BlockSpec((1, tq, SB_GROUP), lambda i, hp, qi: (i, qi, hp))
    return pl.pallas_call(
        functools.partial(_sb_prompt_kernel, tq=tq),
        grid=(b, w // SB_GROUP, t // tq),
        in_specs=[pl.BlockSpec(memory_space=pltpu.SMEM), qo, kv, kv, _full(tri.shape)],
        out_specs=qo,
        out_shape=jax.ShapeDtypeStruct((b, t, w), F32),
        compiler_params=_params("arbitrary", "arbitrary", "arbitrary"),
        name="stick_breaking_prompt",
    )(bias, q, k, v, tri)


SB_ROWS = 8


SB_PAGES = 4


def _sb_sample_kernel(pt_ref, q_ref, bias_ref, kn_ref, vn_ref, *rest, n_new, heads):
    del pt_ref
    kp_refs, vp_refs = rest[:SB_PAGES], rest[SB_PAGES:2 * SB_PAGES]
    tri_ref, bmask_ref, o_ref, carry_ref, acc_ref = rest[2 * SB_PAGES:]
    p = pl.program_id(1)
    page = kn_ref.shape[-1]
    tri = tri_ref[...]

    def process(k_refs, v_refs, valid):
        q = q_ref[0]
        zs = [_dot(q, r[0].astype(BF16)) + bias_ref[...] for r in k_refs]
        carry = carry_ref[...]
        parts = [None] * len(k_refs)
        for s in reversed(range(len(k_refs))):
            lk = _neg_softplus(zs[s])
            if valid is not None:
                lk = jnp.where(valid, lk, 0.0)
            hi, lo = _split(lk)
            a = jnp.exp(zs[s] + _dot(hi, tri) + _dot(lo, tri) + carry)
            if valid is not None:
                a = jnp.where(valid, a, 0.0)
            parts[s] = a.astype(BF16)
            carry = carry + jnp.sum(lk, axis=1, keepdims=True)
        vt = jnp.concatenate([r[0].astype(BF16) for r in v_refs], axis=1)
        acc_ref[...] += _dot_nt(jnp.concatenate(parts, axis=1), vt)
        carry_ref[...] = carry

    @pl.when(p == 0)
    def _():
        carry_ref[...] = jnp.zeros(carry_ref.shape, F32)
        acc_ref[...] = jnp.zeros(acc_ref.shape, F32)
        query = lax.broadcasted_iota(jnp.int32, (LANES, page), 0) % SB_ROWS
        key = lax.broadcasted_iota(jnp.int32, (LANES, page), 1)
        process([kn_ref], [vn_ref], (key < query) & (key < n_new))

    @pl.when(p > 0)
    def _():
        process(kp_refs, vp_refs, None)

    @pl.when(p == pl.num_programs(1) - 1)
    def _():
        own = acc_ref[...] * bmask_ref[...]
        out = own[0:SB_ROWS]
        for h in range(1, heads):
            out = out + own[h * SB_ROWS:(h + 1) * SB_ROWS]
        o_ref[0] = out


def _sb_sample(page_table, q_rows, bias_rows, kt_new, vt_new, cache_kt, cache_vt, tri, bmask, n_new, heads):
    n_seq, n_pages = page_table.shape
    _, w, page = cache_kt.shape

    def page_spec(s):
        def index_map(i, p, pt):
            return (pt[i * n_pages + n_pages - jnp.maximum(p, 1) * SB_PAGES + s], 0, 0)
        return pl.BlockSpec((1, w, page), index_map)

    per_seq = lambda shape: pl.BlockSpec((1,) + shape, lambda i, p, pt: (i, 0, 0))
    const = lambda shape: pl.BlockSpec(shape, lambda i, p, pt: (0, 0))
    pages = [page_spec(s) for s in range(SB_PAGES)]
    grid_spec = pltpu.PrefetchScalarGridSpec(
        num_scalar_prefetch=1,
        grid=(n_seq, n_pages // SB_PAGES + 1),
        in_specs=[per_seq((LANES, w)), const((LANES, page)), per_seq((w, page)), per_seq((w, page))]
                 + pages + pages + [const((page, page)), const((LANES, w))],
        out_specs=per_seq((SB_ROWS, w)),
        scratch_shapes=[pltpu.VMEM((LANES, 1), F32), pltpu.VMEM((LANES, w), F32)],
    )
    return pl.pallas_call(
        functools.partial(_sb_sample_kernel, n_new=n_new, heads=heads),
        grid_spec=grid_spec,
        out_shape=jax.ShapeDtypeStruct((n_seq, SB_ROWS, w), F32),
        compiler_params=_params("arbitrary", "arbitrary"),
        name="stick_breaking_sample",
    )(page_table.reshape(-1), q_rows, bias_rows, kt_new, vt_new, *([cache_kt] * SB_PAGES),
      *([cache_vt] * SB_PAGES), tri, bmask)


def _head_indicator(width):
    head = jnp.arange(width) // HEAD_DIM
    return (head[:, None] == head[None, :]).astype(BF16)


SEQ_TILE = 256
ROUTE_TILE = 256
DENSE_TILE = 512
EXPERT_BLOCK = 512
SB_TILE = 4 * LANES


def _tiles(b, t):
    n = b * t
    return dict(seq=min(t, SEQ_TILE), proj=min(n, SEQ_TILE), flat=t < SEQ_TILE, route=min(n, ROUTE_TILE),
                dense=min(n, DENSE_TILE), sb=min(t, SB_TILE))


def kernel(x_prompt, x_sample, mem_prompt, state_conv, cache_k, cache_v, cache_mem_k, cache_mem_v, page_table,
           g_norm1, g_norm2, g_mem_norm, w_mem_kv, g_mem_q, g_mem_k, w_in_a, w_dw, b_dw, g_conv_ln, b_conv_ln,
           w_out_a, g_kv, w_kv, g_sb_k, w_in_b, g_sb_q, sb_bias, w_out_b, peer_w_q, peer_sub_keys, peer_u,
           peer_v):
    d = x_prompt.shape[-1]
    c_conv = w_dw.shape[-1]
    conv_w = w_dw.shape[1]
    sb_w = w_kv.shape[-1] // 2
    mem_w = w_mem_kv.shape[-1] // 2
    sb_heads = sb_w // HEAD_DIM
    mem_heads = mem_w // HEAD_DIM
    n_mem = mem_prompt.shape[1]
    dec_b, dec_t, _ = x_sample.shape

    row = lambda v: v.reshape(1, -1)
    tile_g = lambda g, heads: jnp.tile(g, heads).reshape(1, -1)
    ind_m = _head_indicator(mem_w)
    ind_sb = _head_indicator(sb_w)
    w_in_a_b = w_in_a[0].astype(BF16)
    w_out_a_b = w_out_a[0].astype(BF16)
    w_in_b_b = w_in_b[0].astype(BF16)
    w_out_b_b = w_out_b[0].astype(BF16)
    w_kv_b = w_kv.astype(BF16)
    wqt = [peer_w_q[l].T.astype(BF16) for l in range(2)]
    keys = [peer_sub_keys[l].astype(BF16) for l in range(2)]
    u_b = [peer_u[l].astype(BF16) for l in range(2)]
    vt_b = [_transpose_bf16(peer_v[l], 512) for l in range(2)]
    tri = (jnp.arange(LANES)[:, None] >= jnp.arange(LANES)[None, :]).astype(BF16)

    b_p, t_p, _ = x_prompt.shape
    mk, mv = _memory_kv(mem_prompt.reshape(b_p * n_mem, d), g_mem_norm.reshape(2, 1, d), w_mem_kv.astype(BF16),
                        jnp.tile(g_mem_k, (1, mem_heads)).reshape(2, 1, mem_w), ind_m)
    mem_k_prompt = mk.reshape(2, b_p, n_mem, mem_heads, HEAD_DIM)
    mem_v_prompt = mv.reshape(2, b_p, n_mem, mem_heads, HEAD_DIM)

    def forward(x, prev, mem_k, mem_v, paged):
        b, t, _ = x.shape
        n = b * t
        tl = _tiles(b, t)
        tt = tl["seq"]
        rows = lambda a: a.reshape(1, n, a.shape[-1]) if tl["flat"] else a
        seqs = lambda a: a.reshape(b, t, a.shape[-1])
        glu, qm = map(seqs, _proj_a(rows(x), row(g_norm1[0]), w_in_a_b, tile_g(g_mem_q[0], mem_heads), ind_m,
                                    c_conv, tl["proj"]))
        c_out = _conv(glu, prev, w_dw[0], row(b_dw[0]), row(g_conv_ln[0]), row(b_conv_ln[0]), tt)
        x = _mix(x, c_out, qm, mem_k[0], mem_v[0], w_out_a_b[:c_conv], w_out_a_b[c_conv:], tt)
        x = _peer(x.reshape(n, d), row(g_norm2[0]), wqt[0], keys[0], u_b[0], vt_b[0],
                  tl["route"], tl["dense"], EXPERT_BLOCK).reshape(b, t, d)
        k, v, q, qm, k_mxu, v_mxu = map(seqs, _proj_b(
            rows(x), row(g_kv), w_kv_b, tile_g(g_sb_k, sb_heads), row(g_norm1[1]), w_in_b_b,
            tile_g(g_sb_q[0], sb_heads), tile_g(g_mem_q[1], mem_heads), ind_sb, ind_m, sb_w, tl["proj"]))
        if paged is None:
            o = _sb_prompt(q, k_mxu, v_mxu, sb_bias[0], tri, tl["sb"])
        else:
            o = paged(q, k, v)
        x = _mix(x, o, qm, mem_k[1], mem_v[1], w_out_b_b[:sb_w], w_out_b_b[sb_w:], tt)
        x = _peer(x.reshape(n, d), row(g_norm2[1]), wqt[1], keys[1], u_b[1], vt_b[1],
                  tl["route"], tl["dense"], EXPERT_BLOCK).reshape(b, t, d)
        return x, glu, k, v

    prev0 = jnp.zeros((b_p, CONV_HALO, c_conv), F32)
    y_prompt, glu_p, k_p, v_p = forward(x_prompt, prev0, mk.reshape(2, b_p, n_mem, mem_w),
                                        mv.reshape(2, b_p, n_mem, mem_w), None)
    conv_prompt = glu_p[None, :, t_p - (conv_w - 1):, :]

    n_pool, page, _, _ = cache_k.shape
    pad_t = SB_ROWS - dec_t
    x_s = jnp.pad(x_sample, ((0, 0), (0, pad_t), (0, 0)))
    prev_s = jnp.pad(state_conv[0], ((0, 0), (CONV_HALO - (conv_w - 1), 0), (0, 0)))
    row_head = jnp.arange(LANES) // SB_ROWS
    row_query = jnp.arange(LANES) % SB_ROWS
    lane_head = jnp.arange(sb_w) // HEAD_DIM
    row_ok = (row_head < sb_heads) & (row_query < dec_t)
    bmask = ((row_head[:, None] == lane_head[None, :]) & row_ok[:, None]).astype(F32)
    bias_rows = jnp.where(row_head < sb_heads, sb_bias[0][jnp.minimum(row_head, sb_heads - 1)], 0.0)
    bias_rows = jnp.broadcast_to(bias_rows[:, None], (LANES, page))
    cache_kt = jnp.transpose(cache_k, (0, 2, 3, 1)).reshape(n_pool, sb_w, page)
    cache_vt = jnp.transpose(cache_v, (0, 2, 3, 1)).reshape(n_pool, sb_w, page)

    def paged(q, k, v):
        q_rows = (jnp.take(q * (HEAD_DIM ** -0.5), row_query, axis=1) * bmask[None]).astype(BF16)
        grow = lambda a: jnp.swapaxes(jnp.pad(a[:, :dec_t], ((0, 0), (0, page - dec_t), (0, 0))), 1, 2)
        return _sb_sample(page_table, q_rows, bias_rows, grow(k), grow(v), cache_kt, cache_vt, tri, bmask,
                          dec_t, sb_heads)

    y_s, glu_s, k_s, v_s = forward(x_s, prev_s, cache_mem_k.reshape(2, dec_b, n_mem, mem_w),
                                   cache_mem_v.reshape(2, dec_b, n_mem, mem_w), paged)
    y_sample = y_s[:, :dec_t]
    conv_sample = jnp.concatenate([state_conv[0][:, dec_t:], glu_s[:, :dec_t]], axis=1)[None]
    shape_kv = lambda a: a.reshape(a.shape[0], a.shape[1], sb_heads, HEAD_DIM)
    return (y_prompt, y_sample, conv_prompt, conv_sample, shape_kv(k_p), shape_kv(v_p),
            shape_kv(k_s[:, :dec_t]), shape_kv(v_s[:, :dec_t]), mem_k_prompt, mem_v_prompt)
```

```python
import functools

import jax
import jax.numpy as jnp
from jax import lax
from jax.experimental import pallas as pl
from jax.experimental.pallas import tpu as pltpu

F32 = jnp.float32
BF16 = jnp.bfloat16

HEAD_DIM = 64
EPS = 1e-6
TOPK = 16
NKEYS = 128
LANES = 128
VMEM_LIMIT = 56 * 1024 * 1024
NEG_INF = float("-inf")


def _params(*sem, **kw):
    return pltpu.CompilerParams(dimension_semantics=sem, vmem_limit_bytes=VMEM_LIMIT, **kw)


def _dot(a, b):
    return jnp.dot(a, b, preferred_element_type=F32)


def _dot_nt(a, b):
    return lax.dot_general(a, b, (((1,), (1,)), ((), ())), preferred_element_type=F32)


def _split(x):
    hi = x.astype(BF16)
    return hi, (x - hi.astype(F32)).astype(BF16)


def _rms(x, g):
    return x * lax.rsqrt(jnp.mean(x * x, axis=-1, keepdims=True) + EPS) * g


def _head_rms(x, g, ind):
    hi, lo = _split(x * x)
    ms = (_dot(hi, ind) + _dot(lo, ind)) * (1.0 / HEAD_DIM)
    return x * lax.rsqrt(ms + EPS) * g


def _neg_softplus(z):
    return -(jnp.maximum(z, 0.0) + jnp.log(1.0 + jnp.exp(-jnp.abs(z))))


def _full(shape):
    n = len(shape)
    return pl.BlockSpec(shape, lambda *_: (0,) * n)


def _memkv_kernel(mem_ref, g_ref, w_ref, gk_ref, ind_ref, k_ref, v_ref, *, width):
    h = _rms(mem_ref[...], g_ref[0]).astype(BF16)
    kv = _dot(h, w_ref[0])
    k_ref[0] = _head_rms(kv[:, :width], gk_ref[0], ind_ref[...])
    v_ref[0] = kv[:, width:]


def _memory_kv(mem, g_norm, w_kv, g_k, ind):
    depth, d, w2 = w_kv.shape
    width = w2 // 2
    n = mem.shape[0]
    return pl.pallas_call(
        functools.partial(_memkv_kernel, width=width),
        grid=(depth,),
        in_specs=[_full((n, d)),
                  pl.BlockSpec((1, 1, d), lambda l: (l, 0, 0)),
                  pl.BlockSpec((1, d, w2), lambda l: (l, 0, 0)),
                  pl.BlockSpec((1, 1, width), lambda l: (l, 0, 0)),
                  _full((width, width))],
        out_specs=[pl.BlockSpec((1, n, width), lambda l: (l, 0, 0))] * 2,
        out_shape=[jax.ShapeDtypeStruct((depth, n, width), F32)] * 2,
        compiler_params=_params("arbitrary"),
        name="memory_kv",
    )(mem, g_norm, w_kv, g_k, ind)


def _proj_a_kernel(x_ref, g_ref, w_ref, gq_ref, ind_ref, glu_ref, qm_ref, *, c):
    h = _rms(x_ref[0], g_ref[...]).astype(BF16)
    p = _dot(h, w_ref[...])
    glu_ref[0] = p[:, :c] * jax.nn.sigmoid(p[:, c:2 * c])
    qm_ref[0] = _head_rms(p[:, 2 * c:], gq_ref[...], ind_ref[...])


def _proj_a(x, g, w, gq, ind, c, tt):
    b, t, d = x.shape
    wm = w.shape[1] - 2 * c
    tile = lambda width: pl.BlockSpec((1, tt, width), lambda i, j: (i, j, 0))
    return pl.pallas_call(
        functools.partial(_proj_a_kernel, c=c),
        grid=(b, t // tt),
        in_specs=[tile(d), _full((1, d)), _full(w.shape), _full((1, wm)), _full((wm, wm))],
        out_specs=[tile(c), tile(wm)],
        out_shape=[jax.ShapeDtypeStruct((b, t, c), F32), jax.ShapeDtypeStruct((b, t, wm), F32)],
        compiler_params=_params("arbitrary", "arbitrary"),
        name="in_proj_a",
    )(x, g, w, gq, ind)


def _proj_b_kernel(x_ref, gkv_ref, wkv_ref, gk_ref, g1_ref, wb_ref, gq_ref, gm_ref, ind_sb_ref, ind_m_ref,
                   k_ref, v_ref, q_ref, qm_ref, kb_ref, vb_ref, *, sb):
    x = x_ref[0]
    kv = _dot(_rms(x, gkv_ref[...]).astype(BF16), wkv_ref[...])
    k = _head_rms(kv[:, :sb], gk_ref[...], ind_sb_ref[...])
    k_ref[0] = k
    v_ref[0] = kv[:, sb:]
    kb_ref[0] = k.astype(BF16)
    vb_ref[0] = kv[:, sb:].astype(BF16)
    p = _dot(_rms(x, g1_ref[...]).astype(BF16), wb_ref[...])
    q_ref[0] = _head_rms(p[:, :sb], gq_ref[...], ind_sb_ref[...])
    qm_ref[0] = _head_rms(p[:, sb:], gm_ref[...], ind_m_ref[...])


def _proj_b(x, gkv, wkv, gk, g1, wb, gq, gm, ind_sb, ind_m, sb, tt):
    b, t, d = x.shape
    wm = wb.shape[1] - sb
    tile = lambda width: pl.BlockSpec((1, tt, width), lambda i, j: (i, j, 0))
    sds = lambda width, dtype=F32: jax.ShapeDtypeStruct((b, t, width), dtype)
    return pl.pallas_call(
        functools.partial(_proj_b_kernel, sb=sb),
        grid=(b, t // tt),
        in_specs=[tile(d), _full((1, d)), _full(wkv.shape), _full((1, sb)), _full((1, d)), _full(wb.shape),
                  _full((1, sb)), _full((1, wm)), _full((sb, sb)), _full((wm, wm))],
        out_specs=[tile(sb), tile(sb), tile(sb), tile(wm), tile(sb), tile(sb)],
        out_shape=[sds(sb), sds(sb), sds(sb), sds(wm), sds(sb, BF16), sds(sb, BF16)],
        compiler_params=_params("arbitrary", "arbitrary"),
        name="in_proj_b",
    )(x, gkv, wkv, gk, g1, wb, gq, gm, ind_sb, ind_m)


CONV_HALO = 32
CONV_ROWS = 64


def _conv_kernel(glu_ref, prev_ref, w_ref, b_ref, g_ref, bl_ref, out_ref, pad_ref, y_ref, *, tt, width, carry):
    c = glu_ref.shape[-1]

    @pl.when(pl.program_id(1) == 0)
    def _():
        pad_ref[0:CONV_HALO, :] = prev_ref[0]

    pad_ref[CONV_HALO:CONV_HALO + tt, :] = glu_ref[0]
    first = CONV_HALO - (width - 1)
    rows = min(CONV_ROWS, tt)
    for r0 in range(0, tt, rows):
        for l0 in range(0, c, LANES):
            acc = jnp.zeros((rows, LANES), F32) + b_ref[:, l0:l0 + LANES]
            for k in range(width):
                acc = acc + w_ref[k:k + 1, l0:l0 + LANES] * pad_ref[first + k + r0:first + k + r0 + rows,
                                                                    l0:l0 + LANES]
            y_ref[r0:r0 + rows, l0:l0 + LANES] = acc
    y = y_ref[...]
    mu = jnp.mean(y, axis=-1, keepdims=True)
    var = jnp.mean(jnp.square(y - mu), axis=-1, keepdims=True)
    yn = (y - mu) * lax.rsqrt(var + EPS) * g_ref[...] + bl_ref[...]
    out_ref[0] = yn * jax.nn.sigmoid(yn)
    if carry:
        pad_ref[0:CONV_HALO, :] = pad_ref[tt:tt + CONV_HALO, :]


def _conv(glu, prev, w, bias, g_ln, b_ln, tt):
    b, t, c = glu.shape
    width = w.shape[0]
    tile = pl.BlockSpec((1, tt, c), lambda i, j: (i, j, 0))
    return pl.pallas_call(
        functools.partial(_conv_kernel, tt=tt, width=width, carry=t > tt),
        grid=(b, t // tt),
        in_specs=[tile, pl.BlockSpec((1, CONV_HALO, c), lambda i, j: (i, 0, 0)), _full(w.shape),
                  _full((1, c)), _full((1, c)), _full((1, c))],
        out_specs=tile,
        out_shape=jax.ShapeDtypeStruct((b, t, c), F32),
        scratch_shapes=[pltpu.VMEM((CONV_HALO + tt, c), F32), pltpu.VMEM((tt, c), F32)],
        compiler_params=_params("arbitrary", "arbitrary"),
        name="conformer_conv",
    )(glu, prev, w, bias, g_ln, b_ln)


def _mix_kernel(x_ref, c_ref, qm_ref, mk_ref, mv_ref, wc_ref, wm_ref, out_ref, *, heads):
    qm = qm_ref[0]
    mk = mk_ref[0].astype(BF16)
    mv = mv_ref[0].astype(BF16)
    head_of_lane = lax.broadcasted_iota(jnp.int32, (1, qm.shape[-1]), 1) // HEAD_DIM
    m_out = jnp.zeros(qm.shape, F32)
    for h in range(heads):
        mine = head_of_lane == h
        s = _dot_nt(jnp.where(mine, qm, 0.0).astype(BF16), mk) * (HEAD_DIM ** -0.5)
        e = jnp.exp(s - jnp.max(s, axis=-1, keepdims=True))
        p = e / jnp.sum(e, axis=-1, keepdims=True)
        m_out = jnp.where(mine, _dot(p.astype(BF16), mv), m_out)
    out_ref[0] = (x_ref[0] + _dot(c_ref[0].astype(BF16), wc_ref[...])
                  + _dot(m_out.astype(BF16), wm_ref[...]))


def _mix(x, c_out, qm, mem_k, mem_v, wc, wm, tt):
    b, t, d = x.shape
    cw, mw = c_out.shape[-1], qm.shape[-1]
    n_mem = mem_k.shape[1]
    tile = lambda width: pl.BlockSpec((1, tt, width), lambda i, j: (i, j, 0))
    mem = pl.BlockSpec((1, n_mem, mw), lambda i, j: (i, 0, 0))
    return pl.pallas_call(
        functools.partial(_mix_kernel, heads=mw // HEAD_DIM),
        grid=(b, t // tt),
        in_specs=[tile(d), tile(cw), tile(mw), mem, mem, _full(wc.shape), _full(wm.shape)],
        out_specs=tile(d),
        out_shape=jax.ShapeDtypeStruct((b, t, d), F32),
        compiler_params=_params("arbitrary", "arbitrary"),
        name="mem_attn_out_proj",
    )(x, c_out, qm, mem_k, mem_v, wc, wm)


def _top16(s, key_idx, stable):
    n = s.shape[0]
    work = s
    rank = jnp.full(s.shape, float(TOPK), F32)
    vals = []
    for r in range(TOPK):
        m = jnp.max(work, axis=0, keepdims=True)
        sel = work == m
        if stable:
            sel = key_idx == jnp.min(jnp.where(sel, key_idx, float(n)), axis=0, keepdims=True)
        rank = jnp.where(sel, float(r), rank)
        work = jnp.where(sel, NEG_INF, work)
        vals.append(m)
    return vals, rank


def _pair_top16(sv1, sv2, cols, stable):
    sub = 8
    row = lax.broadcasted_iota(jnp.int32, (TOPK, cols), 0).astype(F32)
    sv2_col = jnp.zeros((TOPK, cols), F32)
    for c in range(TOPK):
        sv2_col = jnp.where(row == float(c), sv2[c], sv2_col)
    cands, flat = [], []
    for r in range(TOPK):
        nrow = TOPK if r == 0 else sub
        cands.append(jnp.where(row[:nrow] < float(TOPK // (r + 1)), sv1[r] + sv2_col[:nrow], NEG_INF))
        flat.append(row[:nrow] + float(TOPK * r))
    cand = jnp.concatenate(cands, axis=0)
    flat = jnp.concatenate(flat, axis=0)
    top = sv1[0] + sv2[0]
    kept = jnp.zeros(cand.shape, F32)
    z = jnp.zeros((1, cols), F32)
    for _ in range(TOPK):
        m = jnp.max(cand, axis=0, keepdims=True)
        sel = cand == m
        if stable:
            sel = flat == jnp.min(jnp.where(sel, flat, float(TOPK * TOPK)), axis=0, keepdims=True)
        kept = jnp.where(sel, 1.0, kept)
        cand = jnp.where(sel, NEG_INF, cand)
        z = z + jnp.exp(m - top)
    counts, o = [], 0
    for r in range(TOPK):
        nrow = TOPK if r == 0 else sub
        counts.append(jnp.sum(kept[o:o + nrow], axis=0, keepdims=True))
        o += nrow
    return counts, z


def _route_kernel(x_ref, g_ref, wqt_ref, keys_ref, hnt_ref, rho_ref, e2_ref, n1_ref, r1_ref, qt_ref, *, tt, heads):
    hnt = _rms(x_ref[...], g_ref[...]).T.astype(BF16)
    hnt_ref[...] = hnt
    qt_ref[...] = _dot(wqt_ref[...], hnt)
    key_idx = lax.broadcasted_iota(jnp.int32, (NKEYS, LANES), 0).astype(F32)

    def head(h, _):
        base = pl.multiple_of(h * 2 * NKEYS, 2 * NKEYS)

        def select(ls, stable):
            s1 = _dot(keys_ref[h, 0], qt_ref[pl.ds(base, NKEYS), ls].astype(BF16))
            s2 = _dot(keys_ref[h, 1], qt_ref[pl.ds(base + NKEYS, NKEYS), ls].astype(BF16))
            sv1, rank1 = _top16(s1, key_idx, stable)
            sv2, rank2 = _top16(s2, key_idx, stable)
            counts, z = _pair_top16(sv1, sv2, LANES, stable)
            n1 = jnp.zeros(s1.shape, F32)
            for r in range(TOPK):
                n1 = jnp.where(rank1 == float(r), counts[r], n1)
            rho_ref[h, :, ls] = rank2.astype(BF16)
            n1_ref[h, :, ls] = n1
            e2_ref[h, :, ls] = (jnp.exp(s2 - sv2[0]) / z).astype(BF16)
            r1_ref[h, :, ls] = jnp.exp(s1 - sv1[0])
            ranked = lambda rank: jnp.sum(jnp.where(rank < float(TOPK), 1.0, 0.0), axis=0, keepdims=True)
            kept = counts[0]
            for r in range(1, TOPK):
                kept = kept + counts[r]
            return (ranked(rank1) != float(TOPK)) | (ranked(rank2) != float(TOPK)) | (kept != float(TOPK))

        tiles = [slice(l0, l0 + LANES) for l0 in range(0, tt, LANES)]
        tied = [select(ls, stable=False) for ls in tiles]
        for ls, t in zip(tiles, tied):
            @pl.when(jnp.max(jnp.where(t, 1.0, 0.0)) > 0.0)
            def _():
                select(ls, stable=True)
        return 0

    lax.fori_loop(0, heads, head, 0)


def _route(x, g, wqt, keys, tt):
    n, d = x.shape
    heads = keys.shape[0]
    tab = pl.BlockSpec((heads, NKEYS, tt), lambda i: (0, 0, i))
    tab_shape = lambda dtype: jax.ShapeDtypeStruct((heads, NKEYS, n), dtype)
    return pl.pallas_call(
        functools.partial(_route_kernel, tt=tt, heads=heads),
        grid=(n // tt,),
        in_specs=[pl.BlockSpec((tt, d), lambda i: (i, 0)), _full((1, d)), _full(wqt.shape), _full(keys.shape)],
        out_specs=[pl.BlockSpec((d, tt), lambda i: (0, i)), tab, tab, tab, tab],
        out_shape=[jax.ShapeDtypeStruct((d, n), BF16), tab_shape(BF16), tab_shape(BF16), tab_shape(F32),
                   tab_shape(F32)],
        scratch_shapes=[pltpu.VMEM((wqt.shape[0], tt), F32)],
        compiler_params=_params("arbitrary"),
        name="peer_route",
    )(x, g, wqt, keys)


def _transpose_kernel(x_ref, o_ref):
    o_ref[...] = x_ref[...].T.astype(o_ref.dtype)


def _transpose_bf16(x, rows):
    n, d = x.shape
    return pl.pallas_call(
        _transpose_kernel,
        grid=(n // rows,),
        in_specs=[pl.BlockSpec((rows, d), lambda i: (i, 0))],
        out_specs=pl.BlockSpec((d, rows), lambda i: (0, i)),
        out_shape=jax.ShapeDtypeStruct((d, n), BF16),
        compiler_params=_params("arbitrary"),
        name="transpose_cast",
    )(x)


def _dense_kernel(hnt_ref, u_ref, vt_ref, rho_ref, e2_ref, n1_ref, r1_ref, x_ref, out_ref, acc_ref, pre_ref,
                  p_ref, *, eb, heads):
    g = pl.program_id(1)
    n_blocks = 2 * (pl.num_programs(1) - 1)
    tt = acc_ref.shape[1]
    per = eb // NKEYS

    @pl.when(g == 0)
    def _():
        acc_ref[...] = jnp.zeros(acc_ref.shape, F32)
        pre_ref[...] = jnp.zeros(pre_ref.shape, F32)
        p_ref[...] = jnp.zeros(p_ref.shape, BF16)

    for slot in range(2):
        s = 2 * g + slot
        rows = slice(slot * eb, (slot + 1) * eb)
        acc_ref[...] += _dot(vt_ref[:, rows], p_ref[slot])
        blk = jnp.clip(s - 1, 0, n_blocks - 1)
        for ii in range(per):
            i = blk * per + ii
            w = jnp.zeros((NKEYS, tt), BF16)
            for h in range(heads):
                kept = rho_ref[h] < n1_ref[h, pl.ds(i, 1), :].astype(BF16)
                w = w + jnp.where(kept, e2_ref[h], 0) * r1_ref[h, pl.ds(i, 1), :].astype(BF16)
            pre = pre_ref[1 - slot, ii * NKEYS:(ii + 1) * NKEYS, :]
            act = 0.5 * pre * (1.0 + lax.erf(pre * (2.0 ** -0.5)))
            p_ref[1 - slot, ii * NKEYS:(ii + 1) * NKEYS, :] = w * act.astype(BF16)
        pre_ref[slot] = _dot(u_ref[rows, :], hnt_ref[...])

    @pl.when(g == pl.num_programs(1) - 1)
    def _():
        out_ref[...] = x_ref[...] + acc_ref[...].T


def _dense(x, hnt, u, v, rho, e2, n1, r1, tt, eb):
    n, d = x.shape
    n_exp = u.shape[0]
    heads = rho.shape[0]
    pairs = n_exp // (2 * eb)
    tab = pl.BlockSpec((heads, NKEYS, tt), lambda i, s: (0, 0, i))
    return pl.pallas_call(
        functools.partial(_dense_kernel, eb=eb, heads=heads),
        grid=(n // tt, pairs + 1),
        in_specs=[pl.BlockSpec((d, tt), lambda i, s: (0, i)),
                  pl.BlockSpec((2 * eb, d), lambda i, s: (jnp.minimum(s, pairs - 1), 0)),
                  pl.BlockSpec((d, 2 * eb), lambda i, s: (0, jnp.clip(s - 1, 0, pairs - 1))),
                  tab, tab, tab, tab,
                  pl.BlockSpec((tt, d), lambda i, s: (i, 0))],
        out_specs=pl.BlockSpec((tt, d), lambda i, s: (i, 0)),
        out_shape=jax.ShapeDtypeStruct((n, d), F32),
        scratch_shapes=[pltpu.VMEM((d, tt), F32), pltpu.VMEM((2, eb, tt), F32), pltpu.VMEM((2, eb, tt), BF16)],
        compiler_params=_params("arbitrary", "arbitrary"),
        name="peer_dense",
    )(hnt, u, v, rho, e2, n1, r1, x)


def _peer(x, g, wqt, keys, u, v, tt_route, tt_dense, eb):
    hnt, rho, e2, n1, r1 = _route(x, g, wqt, keys, tt_route)
    return _dense(x, hnt, u, v, rho, e2, n1, r1, tt_dense, eb)


def _sb_prompt_kernel(bias_ref, q_ref, k_ref, v_ref, tri_ref, o_ref, *, tq):
    hg = pl.program_id(1)
    qi = pl.program_id(2)
    groups = q_ref.shape[-1] // LANES
    first_head = lax.broadcasted_iota(jnp.int32, (1, LANES), 1) < HEAD_DIM
    strict = (lax.broadcasted_iota(jnp.int32, (tq, tq), 1) < lax.broadcasted_iota(jnp.int32, (tq, tq), 0))
    strict = jnp.concatenate([strict, strict], axis=0)
    tri = tri_ref[...]
    q2, bias = [], []
    for g in range(groups):
        q = q_ref[0, :, g * LANES:(g + 1) * LANES] * (HEAD_DIM ** -0.5)
        q2.append(jnp.concatenate([jnp.where(first_head, q, 0.0), jnp.where(first_head, 0.0, q)],
                                  axis=0).astype(BF16))
        bias.append((bias_ref[(hg * groups + g) * 2], bias_ref[(hg * groups + g) * 2 + 1]))

    def chunk(c, state, diag):
        start = pl.multiple_of(c * tq, tq)
        out = []
        for g in range(groups):
            carry, o_acc = state[g]
            kb = k_ref[0, pl.ds(start, tq), g * LANES:(g + 1) * LANES]
            vb = v_ref[0, pl.ds(start, tq), g * LANES:(g + 1) * LANES]
            z = _dot_nt(q2[g], kb)
            z = jnp.concatenate([z[:tq] + bias[g][0], z[tq:] + bias[g][1]], axis=0)
            lk = _neg_softplus(z)
            if diag:
                lk = jnp.where(strict, lk, 0.0)
            hi, lo = _split(lk)
            parts = []
            for s in reversed(range(tq // LANES)):
                sl = slice(s * LANES, (s + 1) * LANES)
                cum = _dot(hi[:, sl], tri) + _dot(lo[:, sl], tri)
                a = jnp.exp(z[:, sl] + cum + carry)
                if diag:
                    a = jnp.where(strict[:, sl], a, 0.0)
                parts.insert(0, a.astype(BF16))
                carry = carry + jnp.sum(lk[:, sl], axis=1, keepdims=True)
            out.append((carry, o_acc + _dot(jnp.concatenate(parts, axis=1), vb)))
        return tuple(out)

    zero = (jnp.zeros((2 * tq, 1), F32), jnp.zeros((2 * tq, LANES), F32))
    state = chunk(qi, (zero,) * groups, True)
    state = lax.fori_loop(0, qi, lambda jj, st: chunk(qi - 1 - jj, st, False), state)
    for g in range(groups):
        o_acc = state[g][1]
        o_ref[0, :, g * LANES:(g + 1) * LANES] = jnp.where(first_head, o_acc[:tq], o_acc[tq:])


SB_GROUP = 6 * LANES


def _sb_prompt(q, k, v, bias, tri, tq):
    b, t, w = q.shape
    kv = pl.BlockSpec((1, t, SB_GROUP), lambda i, hp, qi: (i, 0, hp), pipeline_mode=pl.Buffered(1))
    qo = pl.BlockSpec((1, tq, SB_GROUP), lambda i, hp, qi: (i, qi, hp))
    return pl.pallas_call(
        functools.partial(_sb_prompt_kernel, tq=tq),
        grid=(b, w // SB_GROUP, t // tq),
        in_specs=[pl.BlockSpec(memory_space=pltpu.SMEM), qo, kv, kv, _full(tri.shape)],
        out_specs=qo,
        out_shape=jax.ShapeDtypeStruct((b, t, w), F32),
        compiler_params=_params("arbitrary", "arbitrary", "arbitrary"),
        name="stick_breaking_prompt",
    )(bias, q, k, v, tri)


SB_ROWS = 8


SB_PAGES = 8


def _sb_sample_kernel(pt_ref, q_ref, bias_ref, kn_ref, vn_ref, *rest, n_new, heads):
    del pt_ref
    kp_refs, vp_refs = rest[:SB_PAGES], rest[SB_PAGES:2 * SB_PAGES]
    tri_ref, bmask_ref, o_ref, carry_ref, acc_ref = rest[2 * SB_PAGES:]
    p = pl.program_id(1)
    page = kn_ref.shape[-1]
    tri = tri_ref[...]

    def process(k_refs, v_refs, valid):
        q = q_ref[0]
        zs = [_dot(q, r[0].astype(BF16)) + bias_ref[...] for r in k_refs]
        carry = carry_ref[...]
        parts = [None] * len(k_refs)
        for s in reversed(range(len(k_refs))):
            lk = _neg_softplus(zs[s])
            if valid is not None:
                lk = jnp.where(valid, lk, 0.0)
            hi, lo = _split(lk)
            a = jnp.exp(zs[s] + _dot(hi, tri) + _dot(lo, tri) + carry)
            if valid is not None:
                a = jnp.where(valid, a, 0.0)
            parts[s] = a.astype(BF16)
            carry = carry + jnp.sum(lk, axis=1, keepdims=True)
        vt = jnp.concatenate([r[0].astype(BF16) for r in v_refs], axis=1)
        acc_ref[...] += _dot_nt(jnp.concatenate(parts, axis=1), vt)
        carry_ref[...] = carry

    @pl.when(p == 0)
    def _():
        carry_ref[...] = jnp.zeros(carry_ref.shape, F32)
        acc_ref[...] = jnp.zeros(acc_ref.shape, F32)
        query = lax.broadcasted_iota(jnp.int32, (LANES, page), 0) % SB_ROWS
        key = lax.broadcasted_iota(jnp.int32, (LANES, page), 1)
        process([kn_ref], [vn_ref], (key < query) & (key < n_new))

    @pl.when(p > 0)
    def _():
        process(kp_refs, vp_refs, None)

    @pl.when(p == pl.num_programs(1) - 1)
    def _():
        own = acc_ref[...] * bmask_ref[...]
        out = own[0:SB_ROWS]
        for h in range(1, heads):
            out = out + own[h * SB_ROWS:(h + 1) * SB_ROWS]
        o_ref[0] = out


def _sb_sample(page_table, q_rows, bias_rows, kt_new, vt_new, cache_kt, cache_vt, tri, bmask, n_new, heads):
    n_seq, n_pages = page_table.shape
    _, w, page = cache_kt.shape
    assert n_pages % SB_PAGES == 0, (n_pages, SB_PAGES)

    def page_spec(s):
        def index_map(i, p, pt):
            return (pt[i * n_pages + n_pages - jnp.maximum(p, 1) * SB_PAGES + s], 0, 0)
        return pl.BlockSpec((1, w, page), index_map)

    per_seq = lambda shape: pl.BlockSpec((1,) + shape, lambda i, p, pt: (i, 0, 0))
    const = lambda shape: pl.BlockSpec(shape, lambda i, p, pt: (0, 0))
    pages = [page_spec(s) for s in range(SB_PAGES)]
    grid_spec = pltpu.PrefetchScalarGridSpec(
        num_scalar_prefetch=1,
        grid=(n_seq, n_pages // SB_PAGES + 1),
        in_specs=[per_seq((LANES, w)), const((LANES, page)), per_seq((w, page)), per_seq((w, page))]
                 + pages + pages + [const((page, page)), const((LANES, w))],
        out_specs=per_seq((SB_ROWS, w)),
        scratch_shapes=[pltpu.VMEM((LANES, 1), F32), pltpu.VMEM((LANES, w), F32)],
    )
    return pl.pallas_call(
        functools.partial(_sb_sample_kernel, n_new=n_new, heads=heads),
        grid_spec=grid_spec,
        out_shape=jax.ShapeDtypeStruct((n_seq, SB_ROWS, w), F32),
        compiler_params=_params("arbitrary", "arbitrary"),
        name="stick_breaking_sample",
    )(page_table.reshape(-1), q_rows, bias_rows, kt_new, vt_new, *([cache_kt] * SB_PAGES),
      *([cache_vt] * SB_PAGES), tri, bmask)


def _head_indicator(width):
    head = jnp.arange(width) // HEAD_DIM
    return (head[:, None] == head[None, :]).astype(BF16)


SEQ_TILE = 256
ROUTE_TILE = 512
DENSE_TILE = 512
EXPERT_BLOCK = 512
SB_TILE = 4 * LANES


def _tiles(b, t):
    n = b * t
    return dict(seq=min(t, SEQ_TILE), proj=min(n, SEQ_TILE), flat=t < SEQ_TILE, route=min(n, ROUTE_TILE),
                dense=min(n, DENSE_TILE), sb=min(t, SB_TILE))


def kernel(x_prompt, x_sample, mem_prompt, state_conv, cache_k, cache_v, cache_mem_k, cache_mem_v, page_table,
           g_norm1, g_norm2, g_mem_norm, w_mem_kv, g_mem_q, g_mem_k, w_in_a, w_dw, b_dw, g_conv_ln, b_conv_ln,
           w_out_a, g_kv, w_kv, g_sb_k, w_in_b, g_sb_q, sb_bias, w_out_b, peer_w_q, peer_sub_keys, peer_u,
           peer_v):
    d = x_prompt.shape[-1]
    c_conv = w_dw.shape[-1]
    conv_w = w_dw.shape[1]
    sb_w = w_kv.shape[-1] // 2
    mem_w = w_mem_kv.shape[-1] // 2
    sb_heads = sb_w // HEAD_DIM
    mem_heads = mem_w // HEAD_DIM
    n_mem = mem_prompt.shape[1]
    dec_b, dec_t, _ = x_sample.shape

    row = lambda v: v.reshape(1, -1)
    tile_g = lambda g, heads: jnp.tile(g, heads).reshape(1, -1)
    ind_m = _head_indicator(mem_w)
    ind_sb = _head_indicator(sb_w)
    w_in_a_b = w_in_a[0].astype(BF16)
    w_out_a_b = w_out_a[0].astype(BF16)
    w_in_b_b = w_in_b[0].astype(BF16)
    w_out_b_b = w_out_b[0].astype(BF16)
    w_kv_b = w_kv.astype(BF16)
    wqt = [peer_w_q[l].T.astype(BF16) for l in range(2)]
    keys = [peer_sub_keys[l].astype(BF16) for l in range(2)]
    u_b = [peer_u[l].astype(BF16) for l in range(2)]
    vt_b = [_transpose_bf16(peer_v[l], 512) for l in range(2)]
    tri = (jnp.arange(LANES)[:, None] >= jnp.arange(LANES)[None, :]).astype(BF16)

    b_p, t_p, _ = x_prompt.shape
    mk, mv = _memory_kv(mem_prompt.reshape(b_p * n_mem, d), g_mem_norm.reshape(2, 1, d), w_mem_kv.astype(BF16),
                        jnp.tile(g_mem_k, (1, mem_heads)).reshape(2, 1, mem_w), ind_m)
    mem_k_prompt = mk.reshape(2, b_p, n_mem, mem_heads, HEAD_DIM)
    mem_v_prompt = mv.reshape(2, b_p, n_mem, mem_heads, HEAD_DIM)

    def forward(x, prev, mem_k, mem_v, paged):
        b, t, _ = x.shape
        n = b * t
        tl = _tiles(b, t)
        tt = tl["seq"]
        rows = lambda a: a.reshape(1, n, a.shape[-1]) if tl["flat"] else a
        seqs = lambda a: a.reshape(b, t, a.shape[-1])
        glu, qm = map(seqs, _proj_a(rows(x), row(g_norm1[0]), w_in_a_b, tile_g(g_mem_q[0], mem_heads), ind_m,
                                    c_conv, tl["proj"]))
        c_out = _conv(glu, prev, w_dw[0], row(b_dw[0]), row(g_conv_ln[0]), row(b_conv_ln[0]), tt)
        x = _mix(x, c_out, qm, mem_k[0], mem_v[0], w_out_a_b[:c_conv], w_out_a_b[c_conv:], tt)
        x = _peer(x.reshape(n, d), row(g_norm2[0]), wqt[0], keys[0], u_b[0], vt_b[0],
                  tl["route"], tl["dense"], EXPERT_BLOCK).reshape(b, t, d)
        k, v, q, qm, k_mxu, v_mxu = map(seqs, _proj_b(
            rows(x), row(g_kv), w_kv_b, tile_g(g_sb_k, sb_heads), row(g_norm1[1]), w_in_b_b,
            tile_g(g_sb_q[0], sb_heads), tile_g(g_mem_q[1], mem_heads), ind_sb, ind_m, sb_w, tl["proj"]))
        if paged is None:
            o = _sb_prompt(q, k_mxu, v_mxu, sb_bias[0], tri, tl["sb"])
        else:
            o = paged(q, k, v)
        x = _mix(x, o, qm, mem_k[1], mem_v[1], w_out_b_b[:sb_w], w_out_b_b[sb_w:], tt)
        x = _peer(x.reshape(n, d), row(g_norm2[1]), wqt[1], keys[1], u_b[1], vt_b[1],
                  tl["route"], tl["dense"], EXPERT_BLOCK).reshape(b, t, d)
        return x, glu, k, v

    prev0 = jnp.zeros((b_p, CONV_HALO, c_conv), F32)
    y_prompt, glu_p, k_p, v_p = forward(x_prompt, prev0, mk.reshape(2, b_p, n_mem, mem_w),
                                        mv.reshape(2, b_p, n_mem, mem_w), None)
    conv_prompt = glu_p[None, :, t_p - (conv_w - 1):, :]

    n_pool, page, _, _ = cache_k.shape
    pad_t = SB_ROWS - dec_t
    x_s = jnp.pad(x_sample, ((0, 0), (0, pad_t), (0, 0)))
    prev_s = jnp.pad(state_conv[0], ((0, 0), (CONV_HALO - (conv_w - 1), 0), (0, 0)))
    row_head = jnp.arange(LANES) // SB_ROWS
    row_query = jnp.arange(LANES) % SB_ROWS
    lane_head = jnp.arange(sb_w) // HEAD_DIM
    row_ok = (row_head < sb_heads) & (row_query < dec_t)
    bmask = ((row_head[:, None] == lane_head[None, :]) & row_ok[:, None]).astype(F32)
    bias_rows = jnp.where(row_head < sb_heads, sb_bias[0][jnp.minimum(row_head, sb_heads - 1)], 0.0)
    bias_rows = jnp.broadcast_to(bias_rows[:, None], (LANES, page))
    cache_kt = jnp.transpose(cache_k, (0, 2, 3, 1)).reshape(n_pool, sb_w, page)
    cache_vt = jnp.transpose(cache_v, (0, 2, 3, 1)).reshape(n_pool, sb_w, page)

    def paged(q, k, v):
        q_rows = (jnp.take(q * (HEAD_DIM ** -0.5), row_query, axis=1) * bmask[None]).astype(BF16)
        grow = lambda a: jnp.swapaxes(jnp.pad(a[:, :dec_t], ((0, 0), (0, page - dec_t), (0, 0))), 1, 2)
        return _sb_sample(page_table, q_rows, bias_rows, grow(k), grow(v), cache_kt, cache_vt, tri, bmask,
                          dec_t, sb_heads)

    y_s, glu_s, k_s, v_s = forward(x_s, prev_s, cache_mem_k.reshape(2, dec_b, n_mem, mem_w),
                                   cache_mem_v.reshape(2, dec_b, n_mem, mem_w), paged)
    y_sample = y_s[:, :dec_t]
    conv_sample = jnp.concatenate([state_conv[0][:, dec_t:], glu_s[:, :dec_t]], axis=1)[None]
    shape_kv = lambda a: a.reshape(a.shape[0], a.shape[1], sb_heads, HEAD_DIM)
    return (y_prompt, y_sample, conv_prompt, conv_sample, shape_kv(k_p), shape_kv(v_p),
            shape_kv(k_s[:, :dec_t]), shape_kv(v_s[:, :dec_t]), mem_k_prompt, mem_v_prompt)
```
